```python
import jax, jax.numpy as jnp
from jax import lax
import numpy as np

D_MODEL = 2048
BATCH = 4
SEQ = 4096
DEPTH = 2

GRID_W = 64
CTX_LEN = 256
MLA_HEADS = 16
Q_LORA = 512
KV_LORA = 256
NOPE_DIM = 128
ROPE_DIM = 64
V_DIM = 128
QK_DIM = NOPE_DIM + ROPE_DIM
ATTN_W = MLA_HEADS * V_DIM
ATTN_SCALE = QK_DIM ** -0.5
ROPE_BASE = 10000.0
Q_BLOCK = 128
CONV_CH = D_MODEL
CONV_K = 3
PEER_HEADS = 8
PEER_KEY_DIM = 256
PEER_HALF = PEER_KEY_DIM // 2
N_KEYS = 128
N_EXPERTS = N_KEYS * N_KEYS
PEER_TOPK = 16
TOKEN_BLOCK = 128
EPS = 1e-6
SPLIT_SIZES = (KV_LORA, ROPE_DIM, Q_LORA, CONV_CH, CONV_CH, CONV_CH, D_MODEL, D_MODEL)
KV_COLS = KV_LORA + ROPE_DIM
IN_COLS = KV_LORA + ROPE_DIM + Q_LORA + 3 * CONV_CH + 2 * D_MODEL

kernel_name = 'hybrid_mla_shortconv_peer_dit'


def rms_norm(x, gain):
    xf = x.astype(jnp.float32)
    y = xf * lax.rsqrt(jnp.mean(xf * xf, axis=-1, keepdims=True) + EPS)
    return (y * gain.astype(jnp.float32)).astype(x.dtype)


def modulate(h, shift, scale):
    return h * (1 + scale) + shift


def ada_params(cvec, w_mod, b_mod):
    return jnp.split(jax.nn.silu(cvec) @ w_mod + b_mod, 6, axis=-1)


def split_cols(p, sizes):
    outs, start = [], 0
    for s in sizes:
        outs.append(p[..., start:start + s])
        start += s
    return outs


def axial_rope(n_tokens):
    rows = n_tokens // GRID_W
    row = jnp.repeat(jnp.arange(rows), GRID_W).astype(jnp.float32)
    col = jnp.tile(jnp.arange(GRID_W), rows).astype(jnp.float32)
    half = ROPE_DIM // 2
    inv_freq = ROPE_BASE ** (-jnp.arange(0, half, 2, dtype=jnp.float32) / half)
    ang = jnp.concatenate([row[:, None] * inv_freq, col[:, None] * inv_freq], axis=-1)
    return jnp.cos(ang)[:, None, :], jnp.sin(ang)[:, None, :]


def apply_rope(x, cos, sin):
    x_nope, x_rot = x[..., :NOPE_DIM], x[..., NOPE_DIM:]
    x1, x2 = x_rot[..., 0::2], x_rot[..., 1::2]
    cos, sin = cos.astype(x.dtype), sin.astype(x.dtype)
    rot = jnp.stack([x1 * cos - x2 * sin, x1 * sin + x2 * cos], axis=-1).reshape(x_rot.shape)
    return jnp.concatenate([x_nope, rot], axis=-1)


def mla_kv(p_kv, p_kr, kv_a_norm, w_kv_up, k_norm, rope):
    b, s, _ = p_kv.shape
    kv = (rms_norm(p_kv, kv_a_norm) @ w_kv_up).reshape(b, s, MLA_HEADS, NOPE_DIM + V_DIM)
    k_rope = jnp.broadcast_to(p_kr[:, :, None, :], (b, s, MLA_HEADS, ROPE_DIM))
    k = rms_norm(jnp.concatenate([kv[..., :NOPE_DIM], k_rope], axis=-1), k_norm)
    if rope is not None:
        k = apply_rope(k, *rope)
    return k, kv[..., NOPE_DIM:]


def mla_q(p_q, q_a_norm, w_q_up, q_norm, rope):
    b, s, _ = p_q.shape
    q = (rms_norm(p_q, q_a_norm) @ w_q_up).reshape(b, s, MLA_HEADS, QK_DIM)
    q = rms_norm(q, q_norm)
    if rope is not None:
        q = apply_rope(q, *rope)
    return q


def softmax_attend(q, k, v):
    s = jnp.einsum('bqhd,bkhd->bhqk', q, k).astype(jnp.float32) * ATTN_SCALE
    p = jax.nn.softmax(s, axis=-1).astype(v.dtype)
    return jnp.einsum('bhqk,bkhd->bqhd', p, v)


def latent_attention(q, k_lat, v_lat, k_ctx, v_ctx):
    b, s, h, _ = q.shape
    k_all = jnp.concatenate([k_lat, k_ctx], axis=1)
    v_all = jnp.concatenate([v_lat, v_ctx], axis=1)
    q_blocks = jnp.moveaxis(q.reshape(b, s // Q_BLOCK, Q_BLOCK, h, QK_DIM), 1, 0)
    out = lax.map(lambda qb: softmax_attend(qb, k_all, v_all), q_blocks)
    return jnp.moveaxis(out, 0, 1).reshape(b, s, h * V_DIM)


def short_conv(b_gate, c_gate, u, conv_w):
    z = c_gate * u
    pad = CONV_K // 2
    zp = jnp.pad(z, ((0, 0), (pad, pad), (0, 0)))
    s = z.shape[1]
    y = sum(conv_w[j] * zp[:, j:j + s] for j in range(CONV_K))
    return b_gate * y


def merge_branches(attn, conv_parts, conv_w, w_attn_out, w_conv_out, w_o):
    b_gate, c_gate, u, g_a, g_c = conv_parts
    y_attn = attn @ w_attn_out
    y_conv = short_conv(b_gate, c_gate, u, conv_w) @ w_conv_out
    return (jax.nn.sigmoid(g_a) * y_attn + jax.nn.sigmoid(g_c) * y_conv) @ w_o


def peer_block(h, w_query, sub_keys, u_tab, v_tab):
    t = h.shape[0]
    q = (h @ w_query).reshape(t, PEER_HEADS, 2, PEER_HALF)
    s = jnp.einsum('thpd,hpnd->thpn', q, sub_keys)
    s_top, i_top = lax.top_k(s, PEER_TOPK)
    cand_s = (s_top[:, :, 0, :, None] + s_top[:, :, 1, None, :]).reshape(t, PEER_HEADS, PEER_TOPK * PEER_TOPK)
    cand_i = (i_top[:, :, 0, :, None] * N_KEYS + i_top[:, :, 1, None, :]).reshape(t, PEER_HEADS, PEER_TOPK * PEER_TOPK)
    best_s, pos = lax.top_k(cand_s, PEER_TOPK)
    idx = jnp.take_along_axis(cand_i, pos, axis=-1)
    g = jax.nn.softmax(best_s.astype(jnp.float32), axis=-1).astype(h.dtype)
    u = jnp.take(u_tab, idx, axis=0)
    a = jax.nn.gelu(jnp.einsum('thkd,td->thk', u, h), approximate=False)
    v = jnp.take(v_tab, idx, axis=0)
    return jnp.einsum('thk,thkd->td', g * a, v)


def peer(h, w_query, sub_keys, u_tab, v_tab):
    b, s, d = h.shape
    blocks = h.reshape(b * s // TOKEN_BLOCK, TOKEN_BLOCK, d)
    out = lax.map(lambda hb: peer_block(hb, w_query, sub_keys, u_tab, v_tab), blocks)
    return out.reshape(b, s, d)


def setup_inputs(seed: int = 0) -> dict:
    key = jax.random.key(seed)
    ks = jax.random.split(key, 24)
    L, D = DEPTH, D_MODEL

    def nrm(k, shape, scale):
        return jax.random.normal(k, shape, jnp.float32) * scale

    return {
        'x': nrm(ks[0], (BATCH, SEQ, D), 1.0),
        'c': nrm(ks[1], (BATCH, D), 1.0),
        'ctx': nrm(ks[2], (BATCH, CTX_LEN, D), 1.0),
        'c_ctx': nrm(ks[3], (D,), 1.0),
        'w_mod': nrm(ks[4], (L, D, 6 * D), 0.5 * D ** -0.5),
        'b_mod': nrm(ks[5], (L, 6 * D), 0.02),
        'ln_mix': 1.0 + nrm(ks[6], (L, D), 0.02),
        'w_in': nrm(ks[7], (L, D, IN_COLS), D ** -0.5),
        'q_a_norm': 1.0 + nrm(ks[8], (L, Q_LORA), 0.02),
        'kv_a_norm': 1.0 + nrm(ks[9], (L, KV_LORA), 0.02),
        'w_q_up': nrm(ks[10], (L, Q_LORA, MLA_HEADS * QK_DIM), Q_LORA ** -0.5),
        'w_kv_up': nrm(ks[11], (L, KV_LORA, MLA_HEADS * (NOPE_DIM + V_DIM)), KV_LORA ** -0.5),
        'q_norm': 1.0 + nrm(ks[12], (L, QK_DIM), 0.02),
        'k_norm': 1.0 + nrm(ks[13], (L, QK_DIM), 0.02),
        'conv_w': nrm(ks[14], (L, CONV_K, CONV_CH), CONV_K ** -0.5),
        'w_attn_out': nrm(ks[15], (L, ATTN_W, D), ATTN_W ** -0.5),
        'w_conv_out': nrm(ks[16], (L, CONV_CH, D), CONV_CH ** -0.5),
        'w_o': nrm(ks[17], (L, D, D), D ** -0.5),
        'ln_ffn': 1.0 + nrm(ks[18], (L, D), 0.02),
        'w_query': nrm(ks[19], (L, D, PEER_HEADS * PEER_KEY_DIM), D ** -0.5),
        'sub_keys': nrm(ks[20], (L, PEER_HEADS, 2, N_KEYS, PEER_HALF), PEER_HALF ** -0.5),
        'u_experts': nrm(ks[21], (L, N_EXPERTS, D), D ** -0.5),
        'v_experts': nrm(ks[22], (L, N_EXPERTS, D), PEER_HEADS ** -0.5),
    }


def reference(x, c, ctx, c_ctx, w_mod, b_mod, ln_mix, w_in, q_a_norm, kv_a_norm, w_q_up, w_kv_up,
              q_norm, k_norm, conv_w, w_attn_out, w_conv_out, w_o, ln_ffn, w_query, sub_keys,
              u_experts, v_experts):
    rope = axial_rope(x.shape[1])
    for l in range(DEPTH):
        last = l == DEPTH - 1
        sh1, sc1, g1, sh2, sc2, g2 = [m[:, None, :] for m in ada_params(c, w_mod[l], b_mod[l])]
        csh1, csc1, cg1, csh2, csc2, cg2 = ada_params(c_ctx, w_mod[l], b_mod[l])

        h_c = modulate(rms_norm(ctx, ln_mix[l]), csh1, csc1)
        if last:
            p_c = split_cols(h_c @ w_in[l][:, :KV_COLS], SPLIT_SIZES[:2])
        else:
            p_c = split_cols(h_c @ w_in[l], SPLIT_SIZES)
        k_c, v_c = mla_kv(p_c[0], p_c[1], kv_a_norm[l], w_kv_up[l], k_norm[l], None)

        h_x = modulate(rms_norm(x, ln_mix[l]), sh1, sc1)
        p_kv, p_kr, p_q, *conv_x = split_cols(h_x @ w_in[l], SPLIT_SIZES)
        k_x, v_x = mla_kv(p_kv, p_kr, kv_a_norm[l], w_kv_up[l], k_norm[l], rope)
        q_x = mla_q(p_q, q_a_norm[l], w_q_up[l], q_norm[l], rope)
        attn_x = latent_attention(q_x, k_x, v_x, k_c, v_c)
        x = x + g1 * merge_branches(attn_x, conv_x, conv_w[l], w_attn_out[l], w_conv_out[l], w_o[l])

        h2 = modulate(rms_norm(x, ln_ffn[l]), sh2, sc2)
        x = x + g2 * peer(h2, w_query[l], sub_keys[l], u_experts[l], v_experts[l])

        if not last:
            q_c = mla_q(p_c[2], q_a_norm[l], w_q_up[l], q_norm[l], None)
            attn_c = softmax_attend(q_c, k_c, v_c).reshape(ctx.shape[0], ctx.shape[1], ATTN_W)
            ctx = ctx + cg1 * merge_branches(attn_c, p_c[3:], conv_w[l], w_attn_out[l], w_conv_out[l], w_o[l])
            h2_c = modulate(rms_norm(ctx, ln_ffn[l]), csh2, csc2)
            ctx = ctx + cg2 * peer(h2_c, w_query[l], sub_keys[l], u_experts[l], v_experts[l])
    return x
```

```python
import functools

import jax
import jax.numpy as jnp
from jax import lax
from jax.experimental import pallas as pl
from jax.experimental.pallas import tpu as pltpu

F32 = jnp.float32
BF16 = jnp.bfloat16

GRID_W = 64
MLA_HEADS = 16
Q_LORA = 512
KV_LORA = 256
NOPE_DIM = 128
ROPE_DIM = 64
V_DIM = 128
QK_DIM = NOPE_DIM + ROPE_DIM
QK_PAD = 256
ATTN_SCALE = QK_DIM ** -0.5
ROPE_BASE = 10000.0
PEER_HEADS = 8
PEER_HALF = 128
N_KEYS = 128
PEER_TOPK = 16
EPS = 1e-6
LANES = 128
BF16_SUBLANES = 16
VMEM_LIMIT = 56 * 1024 * 1024
NEG_INF = float("-inf")
SMALL_COLS = KV_LORA + Q_LORA + LANES


def _params(sem):
    return pltpu.CompilerParams(dimension_semantics=sem, vmem_limit_bytes=VMEM_LIMIT)


def _nt_dot(a, b):
    return lax.dot_general(a, b, (((1,), (1,)), ((), ())), preferred_element_type=F32)


def _ada_kernel(c_ref, w_ref, b_ref, o_ref):
    a = c_ref[...]
    a = a / (1.0 + jnp.exp(-a))
    o_ref[...] = jnp.dot(a.astype(BF16), w_ref[...].astype(BF16), preferred_element_type=F32) + b_ref[...]


def _ada_params(cc, w_mod, b_mod, layer):
    rows, d = cc.shape
    cols = w_mod.shape[2]
    tn = 1024
    return pl.pallas_call(
        _ada_kernel,
        out_shape=jax.ShapeDtypeStruct((rows, cols), F32),
        grid=(cols // tn,),
        in_specs=[
            pl.BlockSpec((rows, d), lambda j: (0, 0)),
            pl.BlockSpec((None, d, tn), lambda j: (layer, 0, j)),
            pl.BlockSpec((None, 1, tn), lambda j: (layer, 0, j)),
        ],
        out_specs=pl.BlockSpec((rows, tn), lambda j: (0, j)),
        compiler_params=_params(("arbitrary",)),
        name="ada_params",
    )(cc, w_mod, b_mod)


def _norm_mod(x, gain, shift, scale):
    ms = jnp.mean(x * x, axis=-1, keepdims=True)
    y = x * lax.rsqrt(ms + EPS) * gain
    return y * (1.0 + scale) + shift


def _inproj_kernel(x_ref, ln_ref, sh_ref, sc_ref, ws_ref, wb_ref,
                   ps_ref, b_ref, z_ref, ga_ref, gc_ref, h_scr, *, tn):
    @pl.when(pl.program_id(1) == 0)
    def _():
        h = _norm_mod(x_ref[...], ln_ref[...], sh_ref[0], sc_ref[0]).astype(BF16)
        h_scr[...] = h
        ps_ref[...] = jnp.dot(h, ws_ref[...], preferred_element_type=F32)

    r = jnp.dot(h_scr[...], wb_ref[...], preferred_element_type=F32)
    b_ref[...] = r[:, 0:tn].astype(BF16)
    z_ref[...] = (r[:, tn:2 * tn] * r[:, 2 * tn:3 * tn]).astype(BF16)
    ga_ref[...] = (1.0 / (1.0 + jnp.exp(-r[:, 3 * tn:4 * tn]))).astype(BF16)
    gc_ref[...] = (1.0 / (1.0 + jnp.exp(-r[:, 4 * tn:5 * tn]))).astype(BF16)


def _in_projection(xs, ln, mod3, w_small, w_big, tm, tn, brow):
    n, d = xs.shape
    nj = d // tn
    wide = pl.BlockSpec((tm, tn), lambda i, j: (i, j))
    out_wide = jax.ShapeDtypeStruct((n, d), BF16)
    return pl.pallas_call(
        functools.partial(_inproj_kernel, tn=tn),
        out_shape=(jax.ShapeDtypeStruct((n, SMALL_COLS), F32), out_wide, out_wide, out_wide, out_wide),
        grid=(n // tm, nj),
        in_specs=[
            pl.BlockSpec((tm, d), lambda i, j: (i, 0)),
            pl.BlockSpec((1, d), lambda i, j: (0, 0)),
            pl.BlockSpec((1, 1, d), lambda i, j: (brow(i) * 6 + 0, 0, 0)),
            pl.BlockSpec((1, 1, d), lambda i, j: (brow(i) * 6 + 1, 0, 0)),
            pl.BlockSpec((d, SMALL_COLS), lambda i, j: (0, 0)),
            pl.BlockSpec((d, 5 * tn), lambda i, j: (0, j)),
        ],
        out_specs=(pl.BlockSpec((tm, SMALL_COLS), lambda i, j: (i, 0)), wide, wide, wide, wide),
        scratch_shapes=[pltpu.VMEM((tm, d), BF16)],
        compiler_params=_params(("arbitrary", "arbitrary")),
        name="in_projection",
    )(xs, ln, mod3, mod3, w_small, w_big)


def _rope(r, c1, s1, s2):
    return r * c1 + pltpu.roll(r, 32, axis=1) * s1 + pltpu.roll(r, 96, axis=1) * s2


def _rms(x, true_width):
    return lax.rsqrt(jnp.sum(x * x, axis=-1, keepdims=True) * (1.0 / true_width) + EPS)


def _qkv_kernel(ps_ref, qan_ref, kvan_ref, wq_ref, wkv_ref, qg_ref, kg_ref, c1_ref, s1_ref, s2_ref,
                q_ref, k_ref, v_ref, qn_scr, kvn_scr, kr_scr, krss_scr):
    @pl.when(pl.program_id(1) == 0)
    def _():
        kv_lat = ps_ref[:, 0:KV_LORA]
        q_lat = ps_ref[:, KV_LORA:KV_LORA + Q_LORA]
        kr = ps_ref[:, KV_LORA + Q_LORA:SMALL_COLS]
        kvn_scr[...] = (kv_lat * _rms(kv_lat, KV_LORA) * kvan_ref[...]).astype(BF16)
        qn_scr[...] = (q_lat * _rms(q_lat, Q_LORA) * qan_ref[...]).astype(BF16)
        krss_scr[...] = jnp.sum(kr * kr, axis=-1, keepdims=True)
        kr_scr[...] = _rope(kr * kg_ref[:, NOPE_DIM:], c1_ref[...], s1_ref[...], s2_ref[...])

    qh = jnp.dot(qn_scr[...], wq_ref[...], preferred_element_type=F32)
    qh = qh * _rms(qh, QK_DIM) * qg_ref[...]
    q_rot = _rope(qh[:, NOPE_DIM:], c1_ref[...], s1_ref[...], s2_ref[...])
    q_ref[...] = (jnp.concatenate([qh[:, :NOPE_DIM], q_rot], axis=-1) * ATTN_SCALE).astype(BF16)

    kvh = jnp.dot(kvn_scr[...], wkv_ref[...], preferred_element_type=F32)
    k_nope = kvh[:, :NOPE_DIM]
    ssq = jnp.sum(k_nope * k_nope, axis=-1, keepdims=True) + krss_scr[...]
    r = lax.rsqrt(ssq * (1.0 / QK_DIM) + EPS)
    k_ref[...] = (jnp.concatenate([k_nope * kg_ref[:, :NOPE_DIM], kr_scr[...]], axis=-1) * r).astype(BF16)
    v_ref[...] = kvh[:, NOPE_DIM:].astype(BF16)


def _qkv_heads(ps, qan, kvan, wq, wkv, qg, kg, c1, s1, s2, tm):
    n = ps.shape[0]
    row = lambda i, h: (i, 0)
    fixed = lambda i, h: (0, 0)
    head = lambda i, h: (h, 0, 0)
    out = lambda i, h: (h, i, 0)
    return pl.pallas_call(
        _qkv_kernel,
        out_shape=(jax.ShapeDtypeStruct((MLA_HEADS, n, QK_PAD), BF16),
                   jax.ShapeDtypeStruct((MLA_HEADS, n, QK_PAD), BF16),
                   jax.ShapeDtypeStruct((MLA_HEADS, n, V_DIM), BF16)),
        grid=(n // tm, MLA_HEADS),
        in_specs=[
            pl.BlockSpec((tm, SMALL_COLS), row),
            pl.BlockSpec((1, Q_LORA), fixed),
            pl.BlockSpec((1, KV_LORA), fixed),
            pl.BlockSpec((None, Q_LORA, QK_PAD), head),
            pl.BlockSpec((None, KV_LORA, NOPE_DIM + V_DIM), head),
            pl.BlockSpec((1, QK_PAD), fixed),
            pl.BlockSpec((1, QK_PAD), fixed),
            pl.BlockSpec((tm, LANES), row),
            pl.BlockSpec((tm, LANES), row),
            pl.BlockSpec((tm, LANES), row),
        ],
        out_specs=(pl.BlockSpec((None, tm, QK_PAD), out),
                   pl.BlockSpec((None, tm, QK_PAD), out),
                   pl.BlockSpec((None, tm, V_DIM), out)),
        scratch_shapes=[pltpu.VMEM((tm, Q_LORA), BF16), pltpu.VMEM((tm, KV_LORA), BF16),
                        pltpu.VMEM((tm, LANES), F32), pltpu.VMEM((tm, 1), F32)],
        compiler_params=_params(("arbitrary", "arbitrary")),
        name="qkv_heads",
    )(ps, qan, kvan, wq, wkv, qg, kg, c1, s1, s2)


def _attn_lat_kernel(q_ref, kl_ref, kc_ref, vl_ref, vc_ref, o_ref):
    q = q_ref[...]
    s_l = _nt_dot(q, kl_ref[...])
    s_c = _nt_dot(q, kc_ref[...])
    m = jnp.maximum(jnp.max(s_l, axis=-1, keepdims=True), jnp.max(s_c, axis=-1, keepdims=True))
    p_l = jnp.exp(s_l - m)
    p_c = jnp.exp(s_c - m)
    denom = jnp.sum(p_l, axis=-1, keepdims=True) + jnp.sum(p_c, axis=-1, keepdims=True)
    o = (jnp.dot(p_l.astype(BF16), vl_ref[...], preferred_element_type=F32)
         + jnp.dot(p_c.astype(BF16), vc_ref[...], preferred_element_type=F32))
    o_ref[...] = (o / denom).astype(BF16)


def _attn_ctx_kernel(q_ref, kc_ref, vc_ref, o_ref):
    s_c = _nt_dot(q_ref[...], kc_ref[...])
    p_c = jnp.exp(s_c - jnp.max(s_c, axis=-1, keepdims=True))
    o = jnp.dot(p_c.astype(BF16), vc_ref[...], preferred_element_type=F32)
    o_ref[...] = (o / jnp.sum(p_c, axis=-1, keepdims=True)).astype(BF16)


def _attention_latent(q, k, v, batch, seq, ctx_len, tq):
    n_lat = batch * seq
    nq = seq // tq
    ctx0 = n_lat // ctx_len
    return pl.pallas_call(
        _attn_lat_kernel,
        out_shape=jax.ShapeDtypeStruct((n_lat, MLA_HEADS * V_DIM), BF16),
        grid=(batch, MLA_HEADS, nq),
        in_specs=[
            pl.BlockSpec((None, tq, QK_PAD), lambda b, h, i: (h, b * nq + i, 0)),
            pl.BlockSpec((None, seq, QK_PAD), lambda b, h, i: (h, b, 0)),
            pl.BlockSpec((None, ctx_len, QK_PAD), lambda b, h, i: (h, ctx0 + b, 0)),
            pl.BlockSpec((None, seq, V_DIM), lambda b, h, i: (h, b, 0)),
            pl.BlockSpec((None, ctx_len, V_DIM), lambda b, h, i: (h, ctx0 + b, 0)),
        ],
        out_specs=pl.BlockSpec((tq, V_DIM), lambda b, h, i: (b * nq + i, h)),
        compiler_params=_params(("arbitrary", "arbitrary", "arbitrary")),
        name="attention_latent",
    )(q, k, k, v, v)


def _attention_context(q, k, v, batch, n_lat, ctx_len):
    ctx0 = n_lat // ctx_len
    blk = lambda b, h: (h, ctx0 + b, 0)
    return pl.pallas_call(
        _attn_ctx_kernel,
        out_shape=jax.ShapeDtypeStruct((batch * ctx_len, MLA_HEADS * V_DIM), BF16),
        grid=(batch, MLA_HEADS),
        in_specs=[
            pl.BlockSpec((None, ctx_len, QK_PAD), blk),
            pl.BlockSpec((None, ctx_len, QK_PAD), blk),
            pl.BlockSpec((None, ctx_len, V_DIM), blk),
        ],
        out_specs=pl.BlockSpec((ctx_len, V_DIM), lambda b, h: (b, h)),
        compiler_params=_params(("arbitrary", "arbitrary")),
        name="attention_context",
    )(q, k, v)


def _merge_kernel(attn_ref, bg_ref, z_ref, zp_ref, zn_ref, hp_ref, hn_ref, cw_ref, wao_ref, wco_ref,
                  ga_ref, gc_ref, m_ref, yc_scr, *, tm):
    @pl.when(pl.program_id(1) == 0)
    def _():
        z = z_ref[...].astype(F32)
        rows = lax.broadcasted_iota(jnp.int32, (tm, 1), 0)
        z_prev = jnp.where(rows == 0, zp_ref[BF16_SUBLANES - 1:BF16_SUBLANES, :].astype(F32),
                           pltpu.roll(z, 1, axis=0)) * hp_ref[...]
        z_next = jnp.where(rows == tm - 1, zn_ref[0:1, :].astype(F32),
                           pltpu.roll(z, tm - 1, axis=0)) * hn_ref[...]
        y = cw_ref[0:1, :] * z_prev + cw_ref[1:2, :] * z + cw_ref[2:3, :] * z_next
        yc_scr[...] = (bg_ref[...].astype(F32) * y).astype(BF16)

    y_attn = jnp.dot(attn_ref[...], wao_ref[...], preferred_element_type=F32)
    y_conv = jnp.dot(yc_scr[...], wco_ref[...], preferred_element_type=F32)
    m_ref[...] = (ga_ref[...].astype(F32) * y_attn + gc_ref[...].astype(F32) * y_conv).astype(BF16)


def _merge(attn, bg, z, has_prev, has_next, conv_w, wao, wco, ga, gc, tm, tn):
    n, d = attn.shape
    halo = tm // BF16_SUBLANES
    last_halo = n // BF16_SUBLANES - 1
    row = lambda i, j: (i, 0)
    return pl.pallas_call(
        functools.partial(_merge_kernel, tm=tm),
        out_shape=jax.ShapeDtypeStruct((n, d), BF16),
        grid=(n // tm, d // tn),
        in_specs=[
            pl.BlockSpec((tm, d), row),
            pl.BlockSpec((tm, d), row),
            pl.BlockSpec((tm, d), row),
            pl.BlockSpec((BF16_SUBLANES, d), lambda i, j: (jnp.maximum(i * halo - 1, 0), 0)),
            pl.BlockSpec((BF16_SUBLANES, d), lambda i, j: (jnp.minimum((i + 1) * halo, last_halo), 0)),
            pl.BlockSpec((tm, 1), row),
            pl.BlockSpec((tm, 1), row),
            pl.BlockSpec((3, d), lambda i, j: (0, 0)),
            pl.BlockSpec((d, tn), lambda i, j: (0, j)),
            pl.BlockSpec((d, tn), lambda i, j: (0, j)),
            pl.BlockSpec((tm, tn), lambda i, j: (i, j)),
            pl.BlockSpec((tm, tn), lambda i, j: (i, j)),
        ],
        out_specs=pl.BlockSpec((tm, tn), lambda i, j: (i, j)),
        scratch_shapes=[pltpu.VMEM((tm, d), BF16)],
        compiler_params=_params(("arbitrary", "arbitrary")),
        name="merge_branches",
    )(attn, bg, z, z, z, has_prev, has_next, conv_w, wao, wco, ga, gc)


def _outproj_kernel(m_ref, wo_ref, x_ref, g_ref, o_ref):
    y = jnp.dot(m_ref[...], wo_ref[...], preferred_element_type=F32)
    o_ref[...] = x_ref[...] + g_ref[0] * y


def _out_projection(m, wo, xs, mod3, tm, tn, brow):
    n, d = xs.shape
    nj = d // tn
    return pl.pallas_call(
        _outproj_kernel,
        out_shape=jax.ShapeDtypeStruct((n, d), F32),
        grid=(n // tm, nj),
        in_specs=[
            pl.BlockSpec((tm, d), lambda i, j: (i, 0)),
            pl.BlockSpec((d, tn), lambda i, j: (0, j)),
            pl.BlockSpec((tm, tn), lambda i, j: (i, j)),
            pl.BlockSpec((1, 1, tn), lambda i, j: (brow(i) * 6 + 2, 0, j)),
        ],
        out_specs=pl.BlockSpec((tm, tn), lambda i, j: (i, j)),
        compiler_params=_params(("arbitrary", "arbitrary")),
        name="out_projection",
    )(m, wo, xs, mod3)


def _top_values(s, count):
    vals = []
    for _ in range(count):
        m = jnp.max(s, axis=0, keepdims=True)
        vals.append(m)
        s = jnp.where(s == m, NEG_INF, s)
    return vals


def _peer_select_kernel(x_ref, ln_ref, sh_ref, sc_ref, wq_ref, sk_ref,
                        h_ref, s1_ref, s2_ref, e1_ref, e2_ref, tau_ref):
    @pl.when(pl.program_id(1) == 0)
    def _():
        h_ref[...] = _norm_mod(x_ref[...], ln_ref[...], sh_ref[0], sc_ref[0]).astype(BF16)

    qp = jnp.dot(h_ref[...], wq_ref[...], preferred_element_type=F32)
    s1 = _nt_dot(sk_ref[0], qp[:, :PEER_HALF].astype(BF16))
    s2 = _nt_dot(sk_ref[1], qp[:, PEER_HALF:].astype(BF16))
    top1 = _top_values(s1, PEER_TOPK)
    top2 = _top_values(s2, PEER_TOPK)
    col2 = jnp.concatenate(top2, axis=0)
    cand = jnp.concatenate([a + col2 for a in top1], axis=0)
    best = _top_values(cand, PEER_TOPK)
    norm = jnp.zeros_like(best[0])
    for b in best:
        norm = norm + jnp.exp(b - best[0])
    s1_ref[...] = s1
    s2_ref[...] = s2
    e1_ref[...] = jnp.exp(s1 - top1[0]) / norm
    e2_ref[...] = jnp.exp(s2 - top2[0])
    tau_ref[...] = best[PEER_TOPK - 1]


def _peer_select(xs, ln, mod3, wq, sk, tm, brow):
    n, d = xs.shape
    score = jax.ShapeDtypeStruct((PEER_HEADS, N_KEYS, n), F32)
    score_spec = pl.BlockSpec((None, N_KEYS, tm), lambda i, h: (h, 0, i))
    return pl.pallas_call(
        _peer_select_kernel,
        out_shape=(jax.ShapeDtypeStruct((n, d), BF16), score, score, score, score,
                   jax.ShapeDtypeStruct((PEER_HEADS, 1, n), F32)),
        grid=(n // tm, PEER_HEADS),
        in_specs=[
            pl.BlockSpec((tm, d), lambda i, h: (i, 0)),
            pl.BlockSpec((1, d), lambda i, h: (0, 0)),
            pl.BlockSpec((1, 1, d), lambda i, h: (brow(i) * 6 + 3, 0, 0)),
            pl.BlockSpec((1, 1, d), lambda i, h: (brow(i) * 6 + 4, 0, 0)),
            pl.BlockSpec((d, 2 * PEER_HALF), lambda i, h: (0, h)),
            pl.BlockSpec((None, 2, N_KEYS, PEER_HALF), lambda i, h: (h, 0, 0, 0)),
        ],
        out_specs=(pl.BlockSpec((tm, d), lambda i, h: (i, 0)), score_spec, score_spec, score_spec, score_spec,
                   pl.BlockSpec((None, 1, tm), lambda i, h: (h, 0, i))),
        compiler_params=_params(("arbitrary", "arbitrary")),
        name="peer_select",
    )(xs, ln, mod3, mod3, wq, sk)


def _gelu(a):
    return 0.5 * a * (1.0 + lax.erf(a * (2.0 ** -0.5)))


def _peer_expert_kernel(h_ref, u_ref, vt_ref, s1_ref, e1_ref, s2_ref, e2_ref, tau_ref, x_ref, g_ref,
                        o_ref, acc_scr, a_scr, p_scr, *, tm, rows_per_block):
    e = pl.program_id(1)

    @pl.when(e == 0)
    def _():
        acc_scr[...] = jnp.zeros_like(acc_scr)

    a_scr[...] = _nt_dot(u_ref[...], h_ref[...])

    for tc in range(tm // LANES):
        cols = slice(tc * LANES, (tc + 1) * LANES)
        for ii in range(rows_per_block):
            weight = jnp.zeros((N_KEYS, LANES), F32)
            for hd in range(PEER_HEADS):
                s1_row = s1_ref[hd, ii:ii + 1, cols]
                e1_row = e1_ref[hd, ii:ii + 1, cols]
                picked = (s2_ref[hd, :, cols] + s1_row) >= tau_ref[hd, :, cols]
                weight = weight + jnp.where(picked, e2_ref[hd, :, cols] * e1_row, 0.0)
            rows = slice(ii * N_KEYS, (ii + 1) * N_KEYS)
            p_scr[rows, cols] = (weight * _gelu(a_scr[rows, cols])).astype(BF16)

    acc_scr[...] += jnp.dot(vt_ref[...], p_scr[...], preferred_element_type=F32)

    @pl.when(e == pl.num_programs(1) - 1)
    def _():
        o_ref[...] = x_ref[...] + g_ref[0] * acc_scr[...].T


def _peer_experts(h2, u_tab, vt_tab, s1, e1, s2, e2, tau, xs, mod3, tm, te, brow):
    n, d = xs.shape
    n_exp = u_tab.shape[0]
    rows_per_block = te // N_KEYS
    token = pl.BlockSpec((tm, d), lambda i, e: (i, 0))
    part = pl.BlockSpec((PEER_HEADS, rows_per_block, tm), lambda i, e: (0, e, i))
    full = pl.BlockSpec((PEER_HEADS, N_KEYS, tm), lambda i, e: (0, 0, i))
    return pl.pallas_call(
        functools.partial(_peer_expert_kernel, tm=tm, rows_per_block=rows_per_block),
        out_shape=jax.ShapeDtypeStruct((n, d), F32),
        grid=(n // tm, n_exp // te),
        in_specs=[
            token,
            pl.BlockSpec((te, d), lambda i, e: (e, 0)),
            pl.BlockSpec((d, te), lambda i, e: (0, e)),
            part, part, full, full,
            pl.BlockSpec((PEER_HEADS, 1, tm), lambda i, e: (0, 0, i)),
            token,
            pl.BlockSpec((1, 1, d), lambda i, e: (brow(i) * 6 + 5, 0, 0)),
        ],
        out_specs=token,
        scratch_shapes=[pltpu.VMEM((d, tm), F32), pltpu.VMEM((te, tm), F32), pltpu.VMEM((te, tm), BF16)],
        compiler_params=_params(("arbitrary", "arbitrary")),
        name="peer_experts",
    )(h2, u_tab, vt_tab, s1, e1, s2, e2, tau, xs, mod3)


def _rope_tables(batch, seq, n_ctx):
    rows = seq // GRID_W
    row = jnp.repeat(jnp.arange(rows), GRID_W).astype(F32)
    col = jnp.tile(jnp.arange(GRID_W), rows).astype(F32)
    half = ROPE_DIM // 2
    inv_freq = ROPE_BASE ** (-jnp.arange(0, half, 2, dtype=F32) / half)
    ang = jnp.concatenate([row[:, None] * inv_freq, col[:, None] * inv_freq], axis=-1)
    cos = jnp.concatenate([jnp.tile(jnp.cos(ang), (batch, 1)), jnp.ones((n_ctx, half), F32)], axis=0)
    sin = jnp.concatenate([jnp.tile(jnp.sin(ang), (batch, 1)), jnp.zeros((n_ctx, half), F32)], axis=0)
    zero = jnp.zeros_like(cos)
    c1 = jnp.concatenate([cos, cos, zero, zero], axis=-1)
    s1 = jnp.concatenate([zero, sin, zero, zero], axis=-1)
    s2 = jnp.concatenate([-sin, zero, zero, zero], axis=-1)
    return c1, s1, s2


def _deinterleave(w):
    return jnp.concatenate([w[..., 0::2], w[..., 1::2]], axis=-1)


def _head_gain(g):
    return jnp.concatenate([g[:NOPE_DIM], _deinterleave(g[NOPE_DIM:]), jnp.zeros((QK_PAD - QK_DIM,), F32)])[None, :]


def kernel(x, c, ctx, c_ctx, w_mod, b_mod, ln_mix, w_in, q_a_norm, kv_a_norm, w_q_up, w_kv_up, q_norm, k_norm,
           conv_w, w_attn_out, w_conv_out, w_o, ln_ffn, w_query, sub_keys, u_experts, v_experts):
    batch, seq, d = x.shape
    ctx_len = ctx.shape[1]
    depth = w_mod.shape[0]
    n_lat, n_ctx = batch * seq, batch * ctx_len
    n = n_lat + n_ctx
    tm = 512
    tn = 256
    te = 1024
    tq = 256
    assert seq % tm == 0 and n_ctx % tm == 0 and n_lat % ctx_len == 0 and seq % GRID_W == 0

    blocks_per_seq = seq // tm
    brow = lambda i: jnp.minimum(i // blocks_per_seq, batch)

    xs = jnp.concatenate([x.reshape(n_lat, d), ctx.reshape(n_ctx, d)], axis=0)
    mod_rows = 8
    cc = jnp.concatenate([c, c_ctx[None, :], jnp.zeros((mod_rows - batch - 1, d), F32)], axis=0)
    c1, s1, s2 = _rope_tables(batch, seq, n_ctx)

    pos = jnp.concatenate([jnp.tile(jnp.arange(seq), batch), jnp.tile(jnp.arange(ctx_len), batch)])
    length = jnp.concatenate([jnp.full((n_lat,), seq), jnp.full((n_ctx,), ctx_len)])
    has_prev = (pos != 0).astype(F32)[:, None]
    has_next = (pos != length - 1).astype(F32)[:, None]

    o_kv, o_kr, o_q = 0, KV_LORA, KV_LORA + ROPE_DIM
    o_b = o_q + Q_LORA

    for l in range(depth):
        wl = w_in[l]
        w_small = jnp.concatenate(
            [wl[:, o_kv:o_kv + KV_LORA], wl[:, o_q:o_q + Q_LORA], _deinterleave(wl[:, o_kr:o_kr + ROPE_DIM]),
             jnp.zeros((d, LANES - ROPE_DIM), F32)], axis=1).astype(BF16)
        w_big = wl[:, o_b:].reshape(d, 5, d // tn, tn).transpose(0, 2, 1, 3).reshape(d, 5 * d).astype(BF16)
        wq = w_q_up[l].reshape(Q_LORA, MLA_HEADS, QK_DIM).transpose(1, 0, 2)
        wq = jnp.concatenate([wq[..., :NOPE_DIM], _deinterleave(wq[..., NOPE_DIM:]),
                              jnp.zeros((MLA_HEADS, Q_LORA, QK_PAD - QK_DIM), F32)], axis=-1).astype(BF16)
        wkv = w_kv_up[l].reshape(KV_LORA, MLA_HEADS, NOPE_DIM + V_DIM).transpose(1, 0, 2).astype(BF16)

        mod3 = _ada_params(cc, w_mod, b_mod.reshape(depth, 1, -1), l).reshape(mod_rows * 6, 1, d)

        ps, bg, z, ga, gc = _in_projection(xs, ln_mix[l][None, :], mod3, w_small, w_big, tm, tn, brow)
        q, k, v = _qkv_heads(ps, q_a_norm[l][None, :], kv_a_norm[l][None, :], wq, wkv,
                             _head_gain(q_norm[l]), _head_gain(k_norm[l]), c1, s1, s2, tm)
        attn = jnp.concatenate([_attention_latent(q, k, v, batch, seq, ctx_len, tq),
                                _attention_context(q, k, v, batch, n_lat, ctx_len)], axis=0)
        m = _merge(attn, bg, z, has_prev, has_next, conv_w[l], w_attn_out[l].astype(BF16),
                   w_conv_out[l].astype(BF16), ga, gc, tm, 512)
        xs = _out_projection(m, w_o[l].astype(BF16), xs, mod3, tm, 512, brow)

        h2, p1, p2, pe1, pe2, tau = _peer_select(xs, ln_ffn[l][None, :], mod3, w_query[l].astype(BF16),
                                                 sub_keys[l].astype(BF16), tm, brow)
        xs = _peer_experts(h2, u_experts[l].astype(BF16), v_experts[l].T.astype(BF16),
                           p1, pe1, p2, pe2, tau, xs, mod3, tm, te, brow)

    return xs[:n_lat].reshape(batch, seq, d)
```

```python
import functools

import jax
import jax.numpy as jnp
from jax import lax
from jax.experimental import pallas as pl
from jax.experimental.pallas import tpu as pltpu

F32 = jnp.float32
BF16 = jnp.bfloat16

GRID_W = 64
MLA_HEADS = 16
Q_LORA = 512
KV_LORA = 256
NOPE_DIM = 128
ROPE_DIM = 64
V_DIM = 128
QK_DIM = NOPE_DIM + ROPE_DIM
QK_PAD = 256
ATTN_SCALE = QK_DIM ** -0.5
LOG2_E = 1.4426950408889634
ROPE_BASE = 10000.0
PEER_HEADS = 8
PEER_HALF = 128
N_KEYS = 128
PEER_TOPK = 16
EPS = 1e-6
LANES = 128
BF16_SUBLANES = 16
VMEM_LIMIT = 56 * 1024 * 1024
NEG_INF = float("-inf")
ROW_GROUPS = 4
VALUE_PIECES = 4
ATTN_SUB = 256
ATTN_KEYS = 1024
SMALL_COLS = KV_LORA + Q_LORA + LANES


def _params(sem):
    return pltpu.CompilerParams(dimension_semantics=sem, vmem_limit_bytes=VMEM_LIMIT)


def _nt_dot(a, b):
    return lax.dot_general(a, b, (((1,), (1,)), ((), ())), preferred_element_type=F32)


def _ada_kernel(c_ref, w_ref, b_ref, o_ref):
    a = c_ref[...]
    a = a / (1.0 + jnp.exp(-a))
    o_ref[...] = jnp.dot(a.astype(BF16), w_ref[...].astype(BF16), preferred_element_type=F32) + b_ref[...]


def _ada_params(cc, w_mod, b_mod, layer):
    rows, d = cc.shape
    cols = w_mod.shape[2]
    tn = 1024
    return pl.pallas_call(
        _ada_kernel,
        out_shape=jax.ShapeDtypeStruct((rows, cols), F32),
        grid=(cols // tn,),
        in_specs=[
            pl.BlockSpec((rows, d), lambda j: (0, 0)),
            pl.BlockSpec((None, d, tn), lambda j: (layer, 0, j)),
            pl.BlockSpec((None, 1, tn), lambda j: (layer, 0, j)),
        ],
        out_specs=pl.BlockSpec((rows, tn), lambda j: (0, j)),
        compiler_params=_params(("arbitrary",)),
        name="ada_params",
    )(cc, w_mod, b_mod)


def _norm_mod(x, gain, shift, scale):
    ms = jnp.mean(x * x, axis=-1, keepdims=True)
    y = x * lax.rsqrt(ms + EPS) * gain
    return y * (1.0 + scale) + shift


def _inproj_kernel(x_ref, ln_ref, sh_ref, sc_ref, ws_ref, wb_ref,
                   ps_ref, b_ref, z_ref, ga_ref, gc_ref, h_scr, *, tn):
    @pl.when(pl.program_id(1) == 0)
    def _():
        h = _norm_mod(x_ref[...], ln_ref[...], sh_ref[0], sc_ref[0]).astype(BF16)
        h_scr[...] = h
        ps_ref[...] = jnp.dot(h, ws_ref[...], preferred_element_type=F32)

    r = jnp.dot(h_scr[...], wb_ref[...], preferred_element_type=F32)
    b_ref[...] = r[:, 0:tn].astype(BF16)
    z_ref[...] = (r[:, tn:2 * tn] * r[:, 2 * tn:3 * tn]).astype(BF16)
    ga_ref[...] = (1.0 / (1.0 + jnp.exp(-r[:, 3 * tn:4 * tn]))).astype(BF16)
    gc_ref[...] = (1.0 / (1.0 + jnp.exp(-r[:, 4 * tn:5 * tn]))).astype(BF16)


def _in_projection(xs, ln, mod3, w_small, w_big, tm, tn, brow):
    n, d = xs.shape
    nj = d // tn
    wide = pl.BlockSpec((tm, tn), lambda i, j: (i, j))
    out_wide = jax.ShapeDtypeStruct((n, d), BF16)
    return pl.pallas_call(
        functools.partial(_inproj_kernel, tn=tn),
        out_shape=(jax.ShapeDtypeStruct((n, SMALL_COLS), F32), out_wide, out_wide, out_wide, out_wide),
        grid=(n // tm, nj),
        in_specs=[
            pl.BlockSpec((tm, d), lambda i, j: (i, 0)),
            pl.BlockSpec((1, d), lambda i, j: (0, 0)),
            pl.BlockSpec((1, 1, d), lambda i, j: (brow(i) * 6 + 0, 0, 0)),
            pl.BlockSpec((1, 1, d), lambda i, j: (brow(i) * 6 + 1, 0, 0)),
            pl.BlockSpec((d, SMALL_COLS), lambda i, j: (0, 0)),
            pl.BlockSpec((d, 5 * tn), lambda i, j: (0, j)),
        ],
        out_specs=(pl.BlockSpec((tm, SMALL_COLS), lambda i, j: (i, 0)), wide, wide, wide, wide),
        scratch_shapes=[pltpu.VMEM((tm, d), BF16)],
        compiler_params=_params(("arbitrary", "arbitrary")),
        name="in_projection",
    )(xs, ln, mod3, mod3, w_small, w_big)


def _rope(r, c1, s1, s2):
    return r * c1 + pltpu.roll(r, 32, axis=1) * s1 + pltpu.roll(r, 96, axis=1) * s2


def _rms(x, true_width):
    return lax.rsqrt(jnp.sum(x * x, axis=-1, keepdims=True) * (1.0 / true_width) + EPS)


def _qkv_kernel(ps_ref, qan_ref, kvan_ref, wq_ref, wkv_ref, qg_ref, kg_ref, c1_ref, s1_ref, s2_ref,
                q_ref, k_ref, v_ref, qn_scr, kvn_scr, kr_scr, krss_scr):
    @pl.when(pl.program_id(1) == 0)
    def _():
        kv_lat = ps_ref[:, 0:KV_LORA]
        q_lat = ps_ref[:, KV_LORA:KV_LORA + Q_LORA]
        kr = ps_ref[:, KV_LORA + Q_LORA:SMALL_COLS]
        kvn_scr[...] = (kv_lat * _rms(kv_lat, KV_LORA) * kvan_ref[...]).astype(BF16)
        qn_scr[...] = (q_lat * _rms(q_lat, Q_LORA) * qan_ref[...]).astype(BF16)
        krss_scr[...] = jnp.sum(kr * kr, axis=-1, keepdims=True)
        kr_scr[...] = _rope(kr * kg_ref[:, NOPE_DIM:], c1_ref[...], s1_ref[...], s2_ref[...])

    qh = jnp.dot(qn_scr[...], wq_ref[...], preferred_element_type=F32)
    qh = qh * _rms(qh, QK_DIM) * qg_ref[...]
    q_rot = _rope(qh[:, NOPE_DIM:], c1_ref[...], s1_ref[...], s2_ref[...])
    q_ref[...] = (jnp.concatenate([qh[:, :NOPE_DIM], q_rot], axis=-1) * (ATTN_SCALE * LOG2_E)).astype(BF16)

    kvh = jnp.dot(kvn_scr[...], wkv_ref[...], preferred_element_type=F32)
    k_nope = kvh[:, :NOPE_DIM]
    ssq = jnp.sum(k_nope * k_nope, axis=-1, keepdims=True) + krss_scr[...]
    r = lax.rsqrt(ssq * (1.0 / QK_DIM) + EPS)
    k_ref[...] = (jnp.concatenate([k_nope * kg_ref[:, :NOPE_DIM], kr_scr[...]], axis=-1) * r).astype(BF16)
    v_ref[...] = kvh[:, NOPE_DIM:].astype(BF16)


def _qkv_heads(ps, qan, kvan, wq, wkv, qg, kg, c1, s1, s2, tm):
    n = ps.shape[0]
    row = lambda i, h: (i, 0)
    fixed = lambda i, h: (0, 0)
    head = lambda i, h: (h, 0, 0)
    out = lambda i, h: (h, i, 0)
    return pl.pallas_call(
        _qkv_kernel,
        out_shape=(jax.ShapeDtypeStruct((MLA_HEADS, n, QK_PAD), BF16),
                   jax.ShapeDtypeStruct((MLA_HEADS, n, QK_PAD), BF16),
                   jax.ShapeDtypeStruct((MLA_HEADS, n, V_DIM), BF16)),
        grid=(n // tm, MLA_HEADS),
        in_specs=[
            pl.BlockSpec((tm, SMALL_COLS), row),
            pl.BlockSpec((1, Q_LORA), fixed),
            pl.BlockSpec((1, KV_LORA), fixed),
            pl.BlockSpec((None, Q_LORA, QK_PAD), head),
            pl.BlockSpec((None, KV_LORA, NOPE_DIM + V_DIM), head),
            pl.BlockSpec((1, QK_PAD), fixed),
            pl.BlockSpec((1, QK_PAD), fixed),
            pl.BlockSpec((tm, LANES), row),
            pl.BlockSpec((tm, LANES), row),
            pl.BlockSpec((tm, LANES), row),
        ],
        out_specs=(pl.BlockSpec((None, tm, QK_PAD), out),
                   pl.BlockSpec((None, tm, QK_PAD), out),
                   pl.BlockSpec((None, tm, V_DIM), out)),
        scratch_shapes=[pltpu.VMEM((tm, Q_LORA), BF16), pltpu.VMEM((tm, KV_LORA), BF16),
                        pltpu.VMEM((tm, LANES), F32), pltpu.VMEM((tm, 1), F32)],
        compiler_params=_params(("arbitrary", "arbitrary")),
        name="qkv_heads",
    )(ps, qan, kvan, wq, wkv, qg, kg, c1, s1, s2)


def _attn_lat_kernel(q_ref, kl_ref, kc_ref, vl_ref, vc_ref, o_ref, s_scr, m_scr, *, tq, seq):
    n_sub = tq // ATTN_SUB
    keys = min(ATTN_KEYS, seq)
    lat_pieces = [(kl_ref, vl_ref, p * keys, keys) for p in range(seq // keys)]
    pieces = lat_pieces + [(kc_ref, vc_ref, 0, kc_ref.shape[0])]
    offsets = [p * keys for p in range(len(lat_pieces))] + [seq]

    def scores(sb):
        slot = sb % 2
        q = q_ref[sb * ATTN_SUB:(sb + 1) * ATTN_SUB, :]
        state = {"m": None}

        def piece(idx):
            def run():
                k_ref, _, start, size = pieces[idx]
                s = _nt_dot(q, k_ref[start:start + size, :])
                s_scr[slot, :, offsets[idx]:offsets[idx] + size] = s
                row_max = jnp.max(s, axis=-1, keepdims=True)
                state["m"] = row_max if state["m"] is None else jnp.maximum(state["m"], row_max)
                if idx == len(pieces) - 1:
                    m_scr[slot] = state["m"]
            return run
        return [piece(idx) for idx in range(len(pieces))]

    def outputs(sb):
        slot = sb % 2
        state = {"acc": None, "den": None}

        def piece(idx):
            def run():
                _, v_ref, start, size = pieces[idx]
                p = jnp.exp2(s_scr[slot, :, offsets[idx]:offsets[idx] + size] - m_scr[slot])
                den = jnp.sum(p, axis=-1, keepdims=True)
                acc = jnp.dot(p.astype(BF16), v_ref[start:start + size, :], preferred_element_type=F32)
                state["den"] = den if state["den"] is None else state["den"] + den
                state["acc"] = acc if state["acc"] is None else state["acc"] + acc
                if idx == len(pieces) - 1:
                    o_ref[sb * ATTN_SUB:(sb + 1) * ATTN_SUB, :] = (state["acc"] / state["den"]).astype(BF16)
            return run
        return [piece(idx) for idx in range(len(pieces))]

    for run in scores(0):
        run()
    for sb in range(n_sub):
        nxt = scores(sb + 1) if sb + 1 < n_sub else []
        cur = outputs(sb)
        for idx in range(len(pieces)):
            if nxt:
                nxt[idx]()
            cur[idx]()


def _attn_ctx_kernel(q_ref, kc_ref, vc_ref, o_ref):
    s_c = _nt_dot(q_ref[...], kc_ref[...])
    p_c = jnp.exp2(s_c - jnp.max(s_c, axis=-1, keepdims=True))
    o = jnp.dot(p_c.astype(BF16), vc_ref[...], preferred_element_type=F32)
    o_ref[...] = (o / jnp.sum(p_c, axis=-1, keepdims=True)).astype(BF16)


def _attention_latent(q, k, v, batch, seq, ctx_len, tq):
    n_lat = batch * seq
    nq = seq // tq
    ctx0 = n_lat // ctx_len
    assert tq % ATTN_SUB == 0 and seq % min(ATTN_KEYS, seq) == 0
    return pl.pallas_call(
        functools.partial(_attn_lat_kernel, tq=tq, seq=seq),
        out_shape=jax.ShapeDtypeStruct((n_lat, MLA_HEADS * V_DIM), BF16),
        grid=(batch, MLA_HEADS, nq),
        in_specs=[
            pl.BlockSpec((None, tq, QK_PAD), lambda b, h, i: (h, b * nq + i, 0)),
            pl.BlockSpec((None, seq, QK_PAD), lambda b, h, i: (h, b, 0)),
            pl.BlockSpec((None, ctx_len, QK_PAD), lambda b, h, i: (h, ctx0 + b, 0)),
            pl.BlockSpec((None, seq, V_DIM), lambda b, h, i: (h, b, 0)),
            pl.BlockSpec((None, ctx_len, V_DIM), lambda b, h, i: (h, ctx0 + b, 0)),
        ],
        out_specs=pl.BlockSpec((tq, V_DIM), lambda b, h, i: (b * nq + i, h)),
        scratch_shapes=[pltpu.VMEM((2, ATTN_SUB, seq + ctx_len), F32), pltpu.VMEM((2, ATTN_SUB, 1), F32)],
        compiler_params=_params(("arbitrary", "arbitrary", "arbitrary")),
        name="attention_latent",
    )(q, k, k, v, v)


def _attention_context(q, k, v, batch, n_lat, ctx_len):
    ctx0 = n_lat // ctx_len
    blk = lambda b, h: (h, ctx0 + b, 0)
    return pl.pallas_call(
        _attn_ctx_kernel,
        out_shape=jax.ShapeDtypeStruct((batch * ctx_len, MLA_HEADS * V_DIM), BF16),
        grid=(batch, MLA_HEADS),
        in_specs=[
            pl.BlockSpec((None, ctx_len, QK_PAD), blk),
            pl.BlockSpec((None, ctx_len, QK_PAD), blk),
            pl.BlockSpec((None, ctx_len, V_DIM), blk),
        ],
        out_specs=pl.BlockSpec((ctx_len, V_DIM), lambda b, h: (b, h)),
        compiler_params=_params(("arbitrary", "arbitrary")),
        name="attention_context",
    )(q, k, v)


def _merge_kernel(attn_ref, bg_ref, z_ref, zp_ref, zn_ref, hp_ref, hn_ref, cw_ref, wao_ref, wco_ref,
                  ga_ref, gc_ref, m_ref, yc_scr, *, tm):
    @pl.when(pl.program_id(1) == 0)
    def _():
        z = z_ref[...].astype(F32)
        rows = lax.broadcasted_iota(jnp.int32, (tm, 1), 0)
        z_prev = jnp.where(rows == 0, zp_ref[BF16_SUBLANES - 1:BF16_SUBLANES, :].astype(F32),
                           pltpu.roll(z, 1, axis=0)) * hp_ref[...]
        z_next = jnp.where(rows == tm - 1, zn_ref[0:1, :].astype(F32),
                           pltpu.roll(z, tm - 1, axis=0)) * hn_ref[...]
        y = cw_ref[0:1, :] * z_prev + cw_ref[1:2, :] * z + cw_ref[2:3, :] * z_next
        yc_scr[...] = (bg_ref[...].astype(F32) * y).astype(BF16)

    y_attn = jnp.dot(attn_ref[...], wao_ref[...], preferred_element_type=F32)
    y_conv = jnp.dot(yc_scr[...], wco_ref[...], preferred_element_type=F32)
    m_ref[...] = (ga_ref[...].astype(F32) * y_attn + gc_ref[...].astype(F32) * y_conv).astype(BF16)


def _merge(attn, bg, z, has_prev, has_next, conv_w, wao, wco, ga, gc, tm, tn):
    n, d = attn.shape
    halo = tm // BF16_SUBLANES
    last_halo = n // BF16_SUBLANES - 1
    row = lambda i, j: (i, 0)
    return pl.pallas_call(
        functools.partial(_merge_kernel, tm=tm),
        out_shape=jax.ShapeDtypeStruct((n, d), BF16),
        grid=(n // tm, d // tn),
        in_specs=[
            pl.BlockSpec((tm, d), row),
            pl.BlockSpec((tm, d), row),
            pl.BlockSpec((tm, d), row),
            pl.BlockSpec((BF16_SUBLANES, d), lambda i, j: (jnp.maximum(i * halo - 1, 0), 0)),
            pl.BlockSpec((BF16_SUBLANES, d), lambda i, j: (jnp.minimum((i + 1) * halo, last_halo), 0)),
            pl.BlockSpec((tm, 1), row),
            pl.BlockSpec((tm, 1), row),
            pl.BlockSpec((3, d), lambda i, j: (0, 0)),
            pl.BlockSpec((d, tn), lambda i, j: (0, j)),
            pl.BlockSpec((d, tn), lambda i, j: (0, j)),
            pl.BlockSpec((tm, tn), lambda i, j: (i, j)),
            pl.BlockSpec((tm, tn), lambda i, j: (i, j)),
        ],
        out_specs=pl.BlockSpec((tm, tn), lambda i, j: (i, j)),
        scratch_shapes=[pltpu.VMEM((tm, d), BF16)],
        compiler_params=_params(("arbitrary", "arbitrary")),
        name="merge_branches",
    )(attn, bg, z, z, z, has_prev, has_next, conv_w, wao, wco, ga, gc)


def _outproj_kernel(m_ref, wo_ref, x_ref, g_ref, o_ref):
    y = jnp.dot(m_ref[...], wo_ref[...], preferred_element_type=F32)
    o_ref[...] = x_ref[...] + g_ref[0] * y


def _out_projection(m, wo, xs, mod3, tm, tn, brow):
    n, d = xs.shape
    nj = d // tn
    return pl.pallas_call(
        _outproj_kernel,
        out_shape=jax.ShapeDtypeStruct((n, d), F32),
        grid=(n // tm, nj),
        in_specs=[
            pl.BlockSpec((tm, d), lambda i, j: (i, 0)),
            pl.BlockSpec((d, tn), lambda i, j: (0, j)),
            pl.BlockSpec((tm, tn), lambda i, j: (i, j)),
            pl.BlockSpec((1, 1, tn), lambda i, j: (brow(i) * 6 + 2, 0, j)),
        ],
        out_specs=pl.BlockSpec((tm, tn), lambda i, j: (i, j)),
        compiler_params=_params(("arbitrary", "arbitrary")),
        name="out_projection",
    )(m, wo, xs, mod3)


def _top_values(s, count):
    vals = []
    for _ in range(count):
        m = jnp.max(s, axis=0, keepdims=True)
        vals.append(m)
        s = jnp.where(s == m, NEG_INF, s)
    return vals


def _peer_select_kernel(x_ref, ln_ref, sh_ref, sc_ref, wq_ref, sk_ref, h_ref, e1_ref, thr_ref, e2_ref):
    @pl.when(pl.program_id(1) == 0)
    def _():
        h_ref[...] = _norm_mod(x_ref[...], ln_ref[...], sh_ref[0], sc_ref[0]).astype(BF16)

    qp = jnp.dot(h_ref[...], wq_ref[...], preferred_element_type=F32)
    s1_all = _nt_dot(sk_ref[0], qp[:, :PEER_HALF].astype(BF16))
    s2_all = _nt_dot(sk_ref[1], qp[:, PEER_HALF:].astype(BF16))
    half = PEER_TOPK // 2
    never = 2.0
    for c in range(s1_all.shape[1] // LANES):
        cols = slice(c * LANES, (c + 1) * LANES)
        s1, s2 = s1_all[:, cols], s2_all[:, cols]
        top1 = _top_values(s1, PEER_TOPK)
        top2 = _top_values(s2, PEER_TOPK)
        col1 = jnp.concatenate(top1, axis=0)
        col2 = jnp.concatenate(top2, axis=0)
        cand = jnp.concatenate([top1[0] + col2] + [top1[k] + col2[:half] for k in range(1, half)]
                               + [col1[half:] + top2[0]], axis=0)
        best = _top_values(cand, PEER_TOPK)
        tau = best[PEER_TOPK - 1]
        norm = jnp.zeros_like(tau)
        for b in best:
            norm = norm + jnp.exp(b - best[0])
        e2_top = jnp.exp(col2 - top2[0])
        thr = jnp.full(s1.shape, never, F32)
        for k in range(PEER_TOPK):
            thr_k = jnp.min(jnp.where(top1[k] + col2 >= tau, e2_top, never), axis=0, keepdims=True)
            thr = jnp.where(s1 == top1[k], thr_k, thr)
        e1_ref[:, cols] = jnp.exp(s1 - top1[0]) * (0.5 / norm)
        thr_ref[:, cols] = thr
        e2_ref[:, cols] = jnp.exp(s2 - top2[0])


def _peer_select(xs, ln, mod3, wq, sk, tm, brow):
    n, d = xs.shape
    score = jax.ShapeDtypeStruct((PEER_HEADS, N_KEYS, n), F32)
    score_spec = pl.BlockSpec((None, N_KEYS, tm), lambda i, h: (h, 0, i))
    return pl.pallas_call(
        _peer_select_kernel,
        out_shape=(jax.ShapeDtypeStruct((n, d), BF16), score, score, score),
        grid=(n // tm, PEER_HEADS),
        in_specs=[
            pl.BlockSpec((tm, d), lambda i, h: (i, 0)),
            pl.BlockSpec((1, d), lambda i, h: (0, 0)),
            pl.BlockSpec((1, 1, d), lambda i, h: (brow(i) * 6 + 3, 0, 0)),
            pl.BlockSpec((1, 1, d), lambda i, h: (brow(i) * 6 + 4, 0, 0)),
            pl.BlockSpec((d, 2 * PEER_HALF), lambda i, h: (0, h)),
            pl.BlockSpec((None, 2, N_KEYS, PEER_HALF), lambda i, h: (h, 0, 0, 0)),
        ],
        out_specs=(pl.BlockSpec((tm, d), lambda i, h: (i, 0)), score_spec, score_spec, score_spec),
        compiler_params=_params(("arbitrary", "arbitrary")),
        name="peer_select",
    )(xs, ln, mod3, mod3, wq, sk)


def _peer_expert_kernel(h_ref, u_ref, vt_ref, e1_ref, thr_ref, e2_ref, x_ref, g_ref,
                        o_ref, acc_scr, a_scr, p_scr, *, tm, rows_per_block):
    e = pl.program_id(1)

    @pl.when(e == 0)
    def _():
        acc_scr[...] = jnp.zeros_like(acc_scr)

    d = acc_scr.shape[0]
    group = rows_per_block // ROW_GROUPS
    sub = group * N_KEYS
    jh = N_KEYS // 2
    tok_half = tm // 2
    d_piece = d // VALUE_PIECES

    def hidden(q, t):
        def run():
            experts = slice(q * sub, (q + 1) * sub)
            toks = slice(t * tok_half, (t + 1) * tok_half)
            a_scr[experts, toks] = _nt_dot(u_ref[experts, :], h_ref[toks, :])
        return run

    def values(q, r):
        def run():
            experts = slice(q * sub, (q + 1) * sub)
            rows = slice(r * d_piece, (r + 1) * d_piece)
            acc_scr[rows, :] += jnp.dot(vt_ref[rows, experts], p_scr[experts, :], preferred_element_type=F32)
        return run

    def weights_tile(q, tc, half):
        def run():
            cols = slice(tc * LANES, (tc + 1) * LANES)
            keys = slice(half * jh, (half + 1) * jh)
            weights = [jnp.zeros((jh, LANES), F32) for _ in range(group)]
            for hd in range(PEER_HEADS):
                e2_t = e2_ref[hd, keys, cols]
                for k in range(group):
                    ii = q * group + k
                    picked = e2_t >= thr_ref[hd, ii:ii + 1, cols]
                    weights[k] = weights[k] + jnp.where(picked, e2_t * e1_ref[hd, ii:ii + 1, cols], 0.0)
            for k in range(group):
                r0 = (q * group + k) * N_KEYS + half * jh
                rows = slice(r0, r0 + jh)
                a = a_scr[rows, cols]
                p_scr[rows, cols] = (weights[k] * (a + a * lax.erf(a * (2.0 ** -0.5)))).astype(BF16)
        return run

    for t in range(2):
        hidden(0, t)()
    for q in range(ROW_GROUPS):
        vpu = [weights_tile(q, tc, half) for tc in range(tm // LANES) for half in range(2)]
        mxu = []
        if q + 1 < ROW_GROUPS:
            mxu += [hidden(q + 1, t) for t in range(2)]
        if q > 0:
            mxu += [values(q - 1, r) for r in range(VALUE_PIECES)]
        for k, tile in enumerate(vpu):
            tile()
            for piece in mxu[k * len(mxu) // len(vpu):(k + 1) * len(mxu) // len(vpu)]:
                piece()
    for r in range(VALUE_PIECES):
        values(ROW_GROUPS - 1, r)()

    @pl.when(e == pl.num_programs(1) - 1)
    def _():
        o_ref[...] = x_ref[...] + g_ref[0] * acc_scr[...].T


def _peer_experts(h2, u_tab, vt_tab, e1, thr, e2, xs, mod3, tm, te, brow):
    n, d = xs.shape
    n_exp = u_tab.shape[0]
    rows_per_block = te // N_KEYS
    token = pl.BlockSpec((tm, d), lambda i, e: (i, 0))
    part = pl.BlockSpec((PEER_HEADS, rows_per_block, tm), lambda i, e: (0, e, i))
    return pl.pallas_call(
        functools.partial(_peer_expert_kernel, tm=tm, rows_per_block=rows_per_block),
        out_shape=jax.ShapeDtypeStruct((n, d), F32),
        grid=(n // tm, n_exp // te),
        in_specs=[
            token,
            pl.BlockSpec((te, d), lambda i, e: (e, 0)),
            pl.BlockSpec((d, te), lambda i, e: (0, e)),
            part, part,
            pl.BlockSpec((PEER_HEADS, N_KEYS, tm), lambda i, e: (0, 0, i)),
            token,
            pl.BlockSpec((1, 1, d), lambda i, e: (brow(i) * 6 + 5, 0, 0)),
        ],
        out_specs=token,
        scratch_shapes=[pltpu.VMEM((d, tm), F32), pltpu.VMEM((te, tm), F32), pltpu.VMEM((te, tm), BF16)],
        compiler_params=_params(("arbitrary", "arbitrary")),
        name="peer_experts",
    )(h2, u_tab, vt_tab, e1, thr, e2, xs, mod3)


def _rope_tables(batch, seq, n_ctx):
    rows = seq // GRID_W
    row = jnp.repeat(jnp.arange(rows), GRID_W).astype(F32)
    col = jnp.tile(jnp.arange(GRID_W), rows).astype(F32)
    half = ROPE_DIM // 2
    inv_freq = ROPE_BASE ** (-jnp.arange(0, half, 2, dtype=F32) / half)
    ang = jnp.concatenate([row[:, None] * inv_freq, col[:, None] * inv_freq], axis=-1)
    cos = jnp.concatenate([jnp.tile(jnp.cos(ang), (batch, 1)), jnp.ones((n_ctx, half), F32)], axis=0)
    sin = jnp.concatenate([jnp.tile(jnp.sin(ang), (batch, 1)), jnp.zeros((n_ctx, half), F32)], axis=0)
    zero = jnp.zeros_like(cos)
    c1 = jnp.concatenate([cos, cos, zero, zero], axis=-1)
    s1 = jnp.concatenate([zero, sin, zero, zero], axis=-1)
    s2 = jnp.concatenate([-sin, zero, zero, zero], axis=-1)
    return c1, s1, s2


def _deinterleave(w):
    return jnp.concatenate([w[..., 0::2], w[..., 1::2]], axis=-1)


def _head_gain(g):
    return jnp.concatenate([g[:NOPE_DIM], _deinterleave(g[NOPE_DIM:]), jnp.zeros((QK_PAD - QK_DIM,), F32)])[None, :]


def kernel(x, c, ctx, c_ctx, w_mod, b_mod, ln_mix, w_in, q_a_norm, kv_a_norm, w_q_up, w_kv_up, q_norm, k_norm,
           conv_w, w_attn_out, w_conv_out, w_o, ln_ffn, w_query, sub_keys, u_experts, v_experts):
    batch, seq, d = x.shape
    ctx_len = ctx.shape[1]
    depth = w_mod.shape[0]
    n_lat, n_ctx = batch * seq, batch * ctx_len
    n = n_lat + n_ctx
    tm = 512
    tn = 256
    te = 1024
    tq = min(1024, seq)
    assert seq % tm == 0 and n_ctx % tm == 0 and n_lat % ctx_len == 0 and seq % GRID_W == 0

    blocks_per_seq = seq // tm
    brow = lambda i: jnp.minimum(i // blocks_per_seq, batch)

    xs = jnp.concatenate([x.reshape(n_lat, d), ctx.reshape(n_ctx, d)], axis=0)
    mod_rows = 8
    cc = jnp.concatenate([c, c_ctx[None, :], jnp.zeros((mod_rows - batch - 1, d), F32)], axis=0)
    c1, s1, s2 = _rope_tables(batch, seq, n_ctx)

    pos = jnp.concatenate([jnp.tile(jnp.arange(seq), batch), jnp.tile(jnp.arange(ctx_len), batch)])
    length = jnp.concatenate([jnp.full((n_lat,), seq), jnp.full((n_ctx,), ctx_len)])
    has_prev = (pos != 0).astype(F32)[:, None]
    has_next = (pos != length - 1).astype(F32)[:, None]

    o_kv, o_kr, o_q = 0, KV_LORA, KV_LORA + ROPE_DIM
    o_b = o_q + Q_LORA

    for l in range(depth):
        wl = w_in[l]
        w_small = jnp.concatenate(
            [wl[:, o_kv:o_kv + KV_LORA], wl[:, o_q:o_q + Q_LORA], _deinterleave(wl[:, o_kr:o_kr + ROPE_DIM]),
             jnp.zeros((d, LANES - ROPE_DIM), F32)], axis=1).astype(BF16)
        w_big = wl[:, o_b:].reshape(d, 5, d // tn, tn).transpose(0, 2, 1, 3).reshape(d, 5 * d).astype(BF16)
        wq = w_q_up[l].reshape(Q_LORA, MLA_HEADS, QK_DIM).transpose(1, 0, 2)
        wq = jnp.concatenate([wq[..., :NOPE_DIM], _deinterleave(wq[..., NOPE_DIM:]),
                              jnp.zeros((MLA_HEADS, Q_LORA, QK_PAD - QK_DIM), F32)], axis=-1).astype(BF16)
        wkv = w_kv_up[l].reshape(KV_LORA, MLA_HEADS, NOPE_DIM + V_DIM).transpose(1, 0, 2).astype(BF16)

        mod3 = _ada_params(cc, w_mod, b_mod.reshape(depth, 1, -1), l).reshape(mod_rows * 6, 1, d)

        ps, bg, z, ga, gc = _in_projection(xs, ln_mix[l][None, :], mod3, w_small, w_big, tm, tn, brow)
        q, k, v = _qkv_heads(ps, q_a_norm[l][None, :], kv_a_norm[l][None, :], wq, wkv,
                             _head_gain(q_norm[l]), _head_gain(k_norm[l]), c1, s1, s2, tm)
        attn = jnp.concatenate([_attention_latent(q, k, v, batch, seq, ctx_len, tq),
                                _attention_context(q, k, v, batch, n_lat, ctx_len)], axis=0)
        m = _merge(attn, bg, z, has_prev, has_next, conv_w[l], w_attn_out[l].astype(BF16),
                   w_conv_out[l].astype(BF16), ga, gc, tm, 512)
        xs = _out_projection(m, w_o[l].astype(BF16), xs, mod3, tm, 512, brow)

        h2, pe1, pthr, pe2 = _peer_select(xs, ln_ffn[l][None, :], mod3, w_query[l].astype(BF16),
                                          sub_keys[l].astype(BF16), tm, brow)
        xs = _peer_experts(h2, u_experts[l].astype(BF16), v_experts[l].T.astype(BF16),
                           pe1, pthr, pe2, xs, mod3, tm, te, brow)

    return xs[:n_lat].reshape(batch, seq, d)
```

```python
import functools
from typing import NamedTuple

import jax
import jax.numpy as jnp
from jax import lax
from jax.experimental import pallas as pl
from jax.experimental.pallas import tpu as pltpu

F32 = jnp.float32
BF16 = jnp.bfloat16

GRID_W = 64
MLA_HEADS = 16
Q_LORA = 512
KV_LORA = 256
NOPE_DIM = 128
ROPE_DIM = 64
V_DIM = 128
QK_DIM = NOPE_DIM + ROPE_DIM
QK_PAD = 256
ATTN_SCALE = QK_DIM ** -0.5
LOG2_E = 1.4426950408889634
ROPE_BASE = 10000.0
PEER_HEADS = 8
PEER_HALF = 128
N_KEYS = 128
PEER_TOPK = 16
EPS = 1e-6
MOD_ROWS = 8
LANES = 128
BF16_SUBLANES = 16
VMEM_LIMIT = 56 * 1024 * 1024
NEG_INF = float("-inf")
ROW_GROUPS = 4
VALUE_PIECES = 4
KEY_SPLIT = 4
ATTN_SUB = 256
ATTN_KEYS = 1024
SMALL_COLS = KV_LORA + Q_LORA + LANES


class Tiles(NamedTuple):
    tm: int
    tn_in: int
    tn_out: int
    te: int
    tq: int


def _tiles(seq, n_ctx):
    tm = 512
    assert seq % tm == 0 and n_ctx % tm == 0
    return Tiles(tm=tm, tn_in=256, tn_out=512, te=ROW_GROUPS * 2 * N_KEYS, tq=min(4 * ATTN_SUB, seq))


def _params(sem, flags=None):
    return pltpu.CompilerParams(dimension_semantics=sem, vmem_limit_bytes=VMEM_LIMIT, flags=flags)


def _nt_dot(a, b):
    return lax.dot_general(a, b, (((1,), (1,)), ((), ())), preferred_element_type=F32)


def _ada_kernel(c_ref, w_ref, b_ref, o_ref):
    a = c_ref[...]
    a = a / (1.0 + jnp.exp(-a))
    o_ref[...] = jnp.dot(a.astype(BF16), w_ref[...].astype(BF16), preferred_element_type=F32) + b_ref[...]


def _ada_params(cc, w_mod, b_mod, layer):
    rows, d = cc.shape
    cols = w_mod.shape[2]
    tn = 1024
    return pl.pallas_call(
        _ada_kernel,
        out_shape=jax.ShapeDtypeStruct((rows, cols), F32),
        grid=(cols // tn,),
        in_specs=[
            pl.BlockSpec((rows, d), lambda j: (0, 0)),
            pl.BlockSpec((None, d, tn), lambda j: (layer, 0, j)),
            pl.BlockSpec((None, 1, tn), lambda j: (layer, 0, j)),
        ],
        out_specs=pl.BlockSpec((rows, tn), lambda j: (0, j)),
        compiler_params=_params(("arbitrary",)),
        name="ada_params",
    )(cc, w_mod, b_mod)


def _norm_mod(x, gain, shift, scale):
    ms = jnp.mean(x * x, axis=-1, keepdims=True)
    y = x * lax.rsqrt(ms + EPS) * gain
    return y * (1.0 + scale) + shift


def _inproj_kernel(x_ref, ln_ref, sh_ref, sc_ref, ws_ref, wb_ref,
                   ps_ref, b_ref, z_ref, ga_ref, gc_ref, h_scr, *, tn):
    @pl.when(pl.program_id(1) == 0)
    def _():
        h = _norm_mod(x_ref[...], ln_ref[0], sh_ref[0], sc_ref[0]).astype(BF16)
        h_scr[...] = h
        ps_ref[...] = jnp.dot(h, ws_ref[...], preferred_element_type=F32)

    r = jnp.dot(h_scr[...], wb_ref[...], preferred_element_type=F32)
    b_ref[...] = r[:, 0:tn].astype(BF16)
    z_ref[...] = (r[:, tn:2 * tn] * r[:, 2 * tn:3 * tn]).astype(BF16)
    ga_ref[...] = (1.0 / (1.0 + jnp.exp(-r[:, 3 * tn:4 * tn]))).astype(BF16)
    gc_ref[...] = (1.0 / (1.0 + jnp.exp(-r[:, 4 * tn:5 * tn]))).astype(BF16)


def _in_projection(xs, ln, mod3, w_small, w_big, layer, t, brow):
    n, d = xs.shape
    tm, tn = t.tm, t.tn_in
    wide = pl.BlockSpec((tm, tn), lambda i, j: (i, j))
    out_wide = jax.ShapeDtypeStruct((n, d), BF16)
    return pl.pallas_call(
        functools.partial(_inproj_kernel, tn=tn),
        out_shape=(jax.ShapeDtypeStruct((n, SMALL_COLS), F32), out_wide, out_wide, out_wide, out_wide),
        grid=(n // tm, d // tn),
        in_specs=[
            pl.BlockSpec((tm, d), lambda i, j: (i, 0)),
            pl.BlockSpec((1, 1, d), lambda i, j: (layer, 0, 0)),
            pl.BlockSpec((1, 1, d), lambda i, j: (brow(i) * 6 + 0, 0, 0)),
            pl.BlockSpec((1, 1, d), lambda i, j: (brow(i) * 6 + 1, 0, 0)),
            pl.BlockSpec((None, d, SMALL_COLS), lambda i, j: (layer, 0, 0)),
            pl.BlockSpec((None, d, 5 * tn), lambda i, j: (layer, 0, j)),
        ],
        out_specs=(pl.BlockSpec((tm, SMALL_COLS), lambda i, j: (i, 0)), wide, wide, wide, wide),
        scratch_shapes=[pltpu.VMEM((tm, d), BF16)],
        compiler_params=_params(("arbitrary", "arbitrary")),
        name="in_projection",
    )(xs, ln, mod3, mod3, w_small, w_big)


def _rope(r, c1, s1, s2):
    return r * c1 + pltpu.roll(r, 32, axis=1) * s1 + pltpu.roll(r, 96, axis=1) * s2


def _rms(x, true_width):
    return lax.rsqrt(jnp.sum(x * x, axis=-1, keepdims=True) * (1.0 / true_width) + EPS)


def _qkv_kernel(ps_ref, qan_ref, kvan_ref, wq_ref, wkv_ref, qg_ref, kg_ref, c1_ref, s1_ref, s2_ref,
                q_ref, k_ref, v_ref):
    c1, s1, s2 = c1_ref[...], s1_ref[...], s2_ref[...]
    kv_lat = ps_ref[:, 0:KV_LORA]
    q_lat = ps_ref[:, KV_LORA:KV_LORA + Q_LORA]
    kr = ps_ref[:, KV_LORA + Q_LORA:SMALL_COLS]
    kvn = (kv_lat * _rms(kv_lat, KV_LORA) * kvan_ref[0]).astype(BF16)
    qn = (q_lat * _rms(q_lat, Q_LORA) * qan_ref[0]).astype(BF16)
    kr_ssq = jnp.sum(kr * kr, axis=-1, keepdims=True)
    qg, kg = qg_ref[0], kg_ref[0]
    kr_rot = _rope(kr * kg[:, NOPE_DIM:], c1, s1, s2)
    for h in range(MLA_HEADS):
        qh = jnp.dot(qn, wq_ref[:, h * QK_PAD:(h + 1) * QK_PAD], preferred_element_type=F32)
        qh = qh * _rms(qh, QK_DIM) * qg
        q_rot = _rope(qh[:, NOPE_DIM:], c1, s1, s2)
        q_ref[h] = (jnp.concatenate([qh[:, :NOPE_DIM], q_rot], axis=-1) * (ATTN_SCALE * LOG2_E)).astype(BF16)

        kvh = jnp.dot(kvn, wkv_ref[:, h * QK_PAD:(h + 1) * QK_PAD], preferred_element_type=F32)
        k_nope = kvh[:, :NOPE_DIM]
        ssq = jnp.sum(k_nope * k_nope, axis=-1, keepdims=True) + kr_ssq
        r = lax.rsqrt(ssq * (1.0 / QK_DIM) + EPS)
        k_ref[h] = (jnp.concatenate([k_nope * kg[:, :NOPE_DIM], kr_rot], axis=-1) * r).astype(BF16)
        v_ref[h] = kvh[:, NOPE_DIM:].astype(BF16)


def _qkv_heads(ps, qan, kvan, wq, wkv, qg, kg, c1, s1, s2, layer, t):
    n = ps.shape[0]
    tm = t.tm
    row = lambda i: (i, 0)
    per_layer = lambda i: (layer, 0, 0)
    out = lambda i: (0, i, 0)
    return pl.pallas_call(
        _qkv_kernel,
        out_shape=(jax.ShapeDtypeStruct((MLA_HEADS, n, QK_PAD), BF16),
                   jax.ShapeDtypeStruct((MLA_HEADS, n, QK_PAD), BF16),
                   jax.ShapeDtypeStruct((MLA_HEADS, n, V_DIM), BF16)),
        grid=(n // tm,),
        in_specs=[
            pl.BlockSpec((tm, SMALL_COLS), row),
            pl.BlockSpec((1, 1, Q_LORA), per_layer),
            pl.BlockSpec((1, 1, KV_LORA), per_layer),
            pl.BlockSpec((None, Q_LORA, MLA_HEADS * QK_PAD), per_layer),
            pl.BlockSpec((None, KV_LORA, MLA_HEADS * QK_PAD), per_layer),
            pl.BlockSpec((1, 1, QK_PAD), per_layer),
            pl.BlockSpec((1, 1, QK_PAD), per_layer),
            pl.BlockSpec((tm, LANES), row),
            pl.BlockSpec((tm, LANES), row),
            pl.BlockSpec((tm, LANES), row),
        ],
        out_specs=(pl.BlockSpec((MLA_HEADS, tm, QK_PAD), out),
                   pl.BlockSpec((MLA_HEADS, tm, QK_PAD), out),
                   pl.BlockSpec((MLA_HEADS, tm, V_DIM), out)),
        compiler_params=_params(("arbitrary",)),
        name="qkv_heads",
    )(ps, qan, kvan, wq, wkv, qg, kg, c1, s1, s2)


def _attn_lat_kernel(q_ref, kl_ref, kc_ref, vl_ref, vc_ref, o_ref, s_scr, m_scr, *, tq, seq):
    n_sub = tq // ATTN_SUB
    keys = min(ATTN_KEYS, seq)
    lat_pieces = [(kl_ref, vl_ref, p * keys, keys) for p in range(seq // keys)]
    pieces = lat_pieces + [(kc_ref, vc_ref, 0, kc_ref.shape[0])]
    offsets = [p * keys for p in range(len(lat_pieces))] + [seq]

    def scores(sb):
        slot = sb % 2
        q = q_ref[sb * ATTN_SUB:(sb + 1) * ATTN_SUB, :]
        state = {"m": None}

        def piece(idx):
            def run():
                k_ref, _, start, size = pieces[idx]
                s = _nt_dot(q, k_ref[start:start + size, :])
                s_scr[slot, :, offsets[idx]:offsets[idx] + size] = s
                row_max = jnp.max(s, axis=-1, keepdims=True)
                state["m"] = row_max if state["m"] is None else jnp.maximum(state["m"], row_max)
                if idx == len(pieces) - 1:
                    m_scr[slot] = state["m"]
            return run
        return [piece(idx) for idx in range(len(pieces))]

    def outputs(sb):
        slot = sb % 2
        state = {"acc": None, "den": None}

        def piece(idx):
            def run():
                _, v_ref, start, size = pieces[idx]
                p = jnp.exp2(s_scr[slot, :, offsets[idx]:offsets[idx] + size] - m_scr[slot])
                den = jnp.sum(p, axis=-1, keepdims=True)
                acc = jnp.dot(p.astype(BF16), v_ref[start:start + size, :], preferred_element_type=F32)
                state["den"] = den if state["den"] is None else state["den"] + den
                state["acc"] = acc if state["acc"] is None else state["acc"] + acc
                if idx == len(pieces) - 1:
                    o_ref[sb * ATTN_SUB:(sb + 1) * ATTN_SUB, :] = (state["acc"] / state["den"]).astype(BF16)
            return run
        return [piece(idx) for idx in range(len(pieces))]

    for run in scores(0):
        run()
    for sb in range(n_sub):
        nxt = scores(sb + 1) if sb + 1 < n_sub else []
        cur = outputs(sb)
        for idx in range(len(pieces)):
            if nxt:
                nxt[idx]()
            cur[idx]()


def _attn_ctx_kernel(q_ref, kc_ref, vc_ref, o_ref):
    s_c = _nt_dot(q_ref[...], kc_ref[...])
    p_c = jnp.exp2(s_c - jnp.max(s_c, axis=-1, keepdims=True))
    o = jnp.dot(p_c.astype(BF16), vc_ref[...], preferred_element_type=F32)
    o_ref[...] = (o / jnp.sum(p_c, axis=-1, keepdims=True)).astype(BF16)


def _attention_latent(q, k, v, batch, seq, ctx_len, t):
    n_lat = batch * seq
    tq = t.tq
    nq = seq // tq
    ctx0 = n_lat // ctx_len
    assert tq % ATTN_SUB == 0 and seq % min(ATTN_KEYS, seq) == 0
    return pl.pallas_call(
        functools.partial(_attn_lat_kernel, tq=tq, seq=seq),
        out_shape=jax.ShapeDtypeStruct((n_lat, MLA_HEADS * V_DIM), BF16),
        grid=(batch, MLA_HEADS, nq),
        in_specs=[
            pl.BlockSpec((None, tq, QK_PAD), lambda b, h, i: (h, b * nq + i, 0)),
            pl.BlockSpec((None, seq, QK_PAD), lambda b, h, i: (h, b, 0)),
            pl.BlockSpec((None, ctx_len, QK_PAD), lambda b, h, i: (h, ctx0 + b, 0)),
            pl.BlockSpec((None, seq, V_DIM), lambda b, h, i: (h, b, 0)),
            pl.BlockSpec((None, ctx_len, V_DIM), lambda b, h, i: (h, ctx0 + b, 0)),
        ],
        out_specs=pl.BlockSpec((tq, V_DIM), lambda b, h, i: (b * nq + i, h)),
        scratch_shapes=[pltpu.VMEM((2, ATTN_SUB, seq + ctx_len), F32), pltpu.VMEM((2, ATTN_SUB, 1), F32)],
        compiler_params=_params(("arbitrary", "arbitrary", "arbitrary")),
        name="attention_latent",
    )(q, k, k, v, v)


def _attention_context(q, k, v, batch, n_lat, ctx_len):
    ctx0 = n_lat // ctx_len
    blk = lambda b, h: (h, ctx0 + b, 0)
    return pl.pallas_call(
        _attn_ctx_kernel,
        out_shape=jax.ShapeDtypeStruct((batch * ctx_len, MLA_HEADS * V_DIM), BF16),
        grid=(batch, MLA_HEADS),
        in_specs=[
            pl.BlockSpec((None, ctx_len, QK_PAD), blk),
            pl.BlockSpec((None, ctx_len, QK_PAD), blk),
            pl.BlockSpec((None, ctx_len, V_DIM), blk),
        ],
        out_specs=pl.BlockSpec((ctx_len, V_DIM), lambda b, h: (b, h)),
        compiler_params=_params(("arbitrary", "arbitrary")),
        name="attention_context",
    )(q, k, v)


def _merge_kernel(attn_ref, bg_ref, z_ref, zp_ref, zn_ref, hp_ref, hn_ref, cw_ref, wao_ref, wco_ref,
                  ga_ref, gc_ref, m_ref, yc_scr, *, tm):
    @pl.when(pl.program_id(1) == 0)
    def _():
        z = z_ref[...].astype(F32)
        rows = lax.broadcasted_iota(jnp.int32, (tm, 1), 0)
        z_prev = jnp.where(rows == 0, zp_ref[BF16_SUBLANES - 1:BF16_SUBLANES, :].astype(F32),
                           pltpu.roll(z, 1, axis=0)) * hp_ref[...]
        z_next = jnp.where(rows == tm - 1, zn_ref[0:1, :].astype(F32),
                           pltpu.roll(z, tm - 1, axis=0)) * hn_ref[...]
        y = cw_ref[0:1, :] * z_prev + cw_ref[1:2, :] * z + cw_ref[2:3, :] * z_next
        yc_scr[...] = (bg_ref[...].astype(F32) * y).astype(BF16)

    y_attn = jnp.dot(attn_ref[...], wao_ref[...], preferred_element_type=F32)
    y_conv = jnp.dot(yc_scr[...], wco_ref[...], preferred_element_type=F32)
    m_ref[...] = (ga_ref[...].astype(F32) * y_attn + gc_ref[...].astype(F32) * y_conv).astype(BF16)


def _merge(attn, bg, z, has_prev, has_next, conv_w, wao, wco, ga, gc, layer, t):
    n, d = attn.shape
    tm, tn = t.tm, t.tn_out
    halo = tm // BF16_SUBLANES
    last_halo = z.shape[0] // BF16_SUBLANES - 1
    row = lambda i, j: (i, 0)
    return pl.pallas_call(
        functools.partial(_merge_kernel, tm=tm),
        out_shape=jax.ShapeDtypeStruct((n, d), BF16),
        grid=(n // tm, d // tn),
        in_specs=[
            pl.BlockSpec((tm, d), row),
            pl.BlockSpec((tm, d), row),
            pl.BlockSpec((tm, d), row),
            pl.BlockSpec((BF16_SUBLANES, d), lambda i, j: (jnp.maximum(i * halo - 1, 0), 0)),
            pl.BlockSpec((BF16_SUBLANES, d), lambda i, j: (jnp.minimum((i + 1) * halo, last_halo), 0)),
            pl.BlockSpec((tm, 1), row),
            pl.BlockSpec((tm, 1), row),
            pl.BlockSpec((None, 3, d), lambda i, j: (layer, 0, 0)),
            pl.BlockSpec((None, d, tn), lambda i, j: (layer, 0, j)),
            pl.BlockSpec((None, d, tn), lambda i, j: (layer, 0, j)),
            pl.BlockSpec((tm, tn), lambda i, j: (i, j)),
            pl.BlockSpec((tm, tn), lambda i, j: (i, j)),
        ],
        out_specs=pl.BlockSpec((tm, tn), lambda i, j: (i, j)),
        scratch_shapes=[pltpu.VMEM((tm, d), BF16)],
        compiler_params=_params(("arbitrary", "arbitrary")),
        name="merge_branches",
    )(attn, bg, z, z, z, has_prev, has_next, conv_w, wao, wco, ga, gc)


def _outproj_kernel(m_ref, wo_ref, x_ref, g_ref, o_ref):
    y = jnp.dot(m_ref[...], wo_ref[...], preferred_element_type=F32)
    o_ref[...] = x_ref[...] + g_ref[0] * y


def _out_projection(m, wo, xs, mod3, layer, t, brow):
    n, d = m.shape
    tm, tn = t.tm, t.tn_out
    return pl.pallas_call(
        _outproj_kernel,
        out_shape=jax.ShapeDtypeStruct((n, d), F32),
        grid=(n // tm, d // tn),
        in_specs=[
            pl.BlockSpec((tm, d), lambda i, j: (i, 0)),
            pl.BlockSpec((None, d, tn), lambda i, j: (layer, 0, j)),
            pl.BlockSpec((tm, tn), lambda i, j: (i, j)),
            pl.BlockSpec((1, 1, tn), lambda i, j: (brow(i) * 6 + 2, 0, j)),
        ],
        out_specs=pl.BlockSpec((tm, tn), lambda i, j: (i, j)),
        compiler_params=_params(("arbitrary", "arbitrary")),
        name="out_projection",
    )(m, wo, xs, mod3)


def _top_values(s, count):
    vals = []
    for _ in range(count):
        m = jnp.max(s, axis=0, keepdims=True)
        vals.append(m)
        s = jnp.where(s == m, NEG_INF, s)
    return vals


def _peer_select_kernel(x_ref, ln_ref, sh_ref, sc_ref, wq_ref, sk_ref, h_ref, e1_ref, thr_ref, e2_ref):
    @pl.when(pl.program_id(1) == 0)
    def _():
        h_ref[...] = _norm_mod(x_ref[...], ln_ref[0], sh_ref[0], sc_ref[0]).astype(BF16)

    qp = jnp.dot(h_ref[...], wq_ref[...], preferred_element_type=F32)
    s1_all = _nt_dot(sk_ref[0], qp[:, :PEER_HALF].astype(BF16))
    s2_all = _nt_dot(sk_ref[1], qp[:, PEER_HALF:].astype(BF16))
    half = PEER_TOPK // 2
    never = 2.0
    for c in range(s1_all.shape[1] // LANES):
        cols = slice(c * LANES, (c + 1) * LANES)
        s1, s2 = s1_all[:, cols], s2_all[:, cols]
        top1 = _top_values(s1, PEER_TOPK)
        top2 = _top_values(s2, PEER_TOPK)
        col1 = jnp.concatenate(top1, axis=0)
        col2 = jnp.concatenate(top2, axis=0)
        cand = jnp.concatenate([top1[0] + col2] + [top1[k] + col2[:half] for k in range(1, half)]
                               + [col1[half:] + top2[0]], axis=0)
        best = _top_values(cand, PEER_TOPK)
        tau = best[PEER_TOPK - 1]
        norm = jnp.zeros_like(tau)
        for b in best:
            norm = norm + jnp.exp(b - best[0])
        e2_top = jnp.exp(col2 - top2[0])
        thr = jnp.full(s1.shape, never, F32)
        for k in range(PEER_TOPK):
            thr_k = jnp.min(jnp.where(top1[k] + col2 >= tau, e2_top, never), axis=0, keepdims=True)
            thr = jnp.where(s1 == top1[k], thr_k, thr)
        e1_ref[:, cols] = jnp.exp(s1 - top1[0]) * (0.5 / norm)
        thr_ref[:, cols] = thr
        e2_ref[:, cols] = jnp.exp(s2 - top2[0])


def _peer_select(xs, ln, mod3, wq, sk, layer, t, brow):
    n, d = xs.shape
    tm = t.tm
    score = jax.ShapeDtypeStruct((PEER_HEADS, N_KEYS, n), F32)
    score_spec = pl.BlockSpec((None, N_KEYS, tm), lambda i, h: (h, 0, i))
    return pl.pallas_call(
        _peer_select_kernel,
        out_shape=(jax.ShapeDtypeStruct((n, d), BF16), score, score, score),
        grid=(n // tm, PEER_HEADS),
        in_specs=[
            pl.BlockSpec((tm, d), lambda i, h: (i, 0)),
            pl.BlockSpec((1, 1, d), lambda i, h: (layer, 0, 0)),
            pl.BlockSpec((1, 1, d), lambda i, h: (brow(i) * 6 + 3, 0, 0)),
            pl.BlockSpec((1, 1, d), lambda i, h: (brow(i) * 6 + 4, 0, 0)),
            pl.BlockSpec((None, d, 2 * PEER_HALF), lambda i, h: (layer, 0, h)),
            pl.BlockSpec((None, None, 2, N_KEYS, PEER_HALF), lambda i, h: (layer, h, 0, 0, 0)),
        ],
        out_specs=(pl.BlockSpec((tm, d), lambda i, h: (i, 0)), score_spec, score_spec, score_spec),
        compiler_params=_params(("arbitrary", "arbitrary")),
        name="peer_select",
    )(xs, ln, mod3, mod3, wq, sk)


def _peer_expert_kernel(h_ref, u_ref, vt_ref, e1_ref, thr_ref, e2_ref, x_ref, g_ref,
                        o_ref, acc_scr, a_scr, p_scr, *, tm, rows_per_block):
    e = pl.program_id(1)

    @pl.when(e == 0)
    def _():
        acc_scr[...] = jnp.zeros_like(acc_scr)

    d = acc_scr.shape[0]
    group = rows_per_block // ROW_GROUPS
    sub = group * N_KEYS
    jh = N_KEYS // KEY_SPLIT
    d_piece = d // VALUE_PIECES

    def hidden(q):
        def run():
            experts = slice(q * sub, (q + 1) * sub)
            a_scr[experts, :] = _nt_dot(u_ref[experts, :], h_ref[...])
        return run

    def values(q, r):
        def run():
            experts = slice(q * sub, (q + 1) * sub)
            rows = slice(r * d_piece, (r + 1) * d_piece)
            acc_scr[rows, :] += jnp.dot(vt_ref[rows, experts], p_scr[experts, :], preferred_element_type=F32)
        return run

    def weights_tile(q, tc, half):
        def run():
            cols = slice(tc * LANES, (tc + 1) * LANES)
            keys = slice(half * jh, (half + 1) * jh)
            weights = [jnp.zeros((jh, LANES), F32) for _ in range(group)]
            for hd in range(PEER_HEADS):
                e2_t = e2_ref[hd, keys, cols]
                for k in range(group):
                    ii = q * group + k
                    picked = e2_t >= thr_ref[hd, ii:ii + 1, cols]
                    weights[k] = weights[k] + jnp.where(picked, e2_t * e1_ref[hd, ii:ii + 1, cols], 0.0)
            for k in range(group):
                r0 = (q * group + k) * N_KEYS + half * jh
                rows = slice(r0, r0 + jh)
                a = a_scr[rows, cols]
                p_scr[rows, cols] = (weights[k] * (a + a * lax.erf(a * (2.0 ** -0.5)))).astype(BF16)
        return run

    hidden(0)()
    for q in range(ROW_GROUPS):
        vpu = [weights_tile(q, tc, half) for tc in range(tm // LANES) for half in range(KEY_SPLIT)]
        mxu = []
        if q + 1 < ROW_GROUPS:
            mxu.append(hidden(q + 1))
        if q > 0:
            mxu += [values(q - 1, r) for r in range(VALUE_PIECES)]
        for k, tile in enumerate(vpu):
            tile()
            for piece in mxu[k * len(mxu) // len(vpu):(k + 1) * len(mxu) // len(vpu)]:
                piece()
    for r in range(VALUE_PIECES):
        values(ROW_GROUPS - 1, r)()

    @pl.when(e == pl.num_programs(1) - 1)
    def _():
        o_ref[...] = x_ref[...] + g_ref[0] * acc_scr[...].T


def _peer_experts(h2, u_tab, vt_tab, e1, thr, e2, xs, mod3, layer, t, brow):
    n, d = xs.shape
    tm, te = t.tm, t.te
    n_blocks = vt_tab.shape[1]
    rows_per_block = te // N_KEYS
    assert rows_per_block % ROW_GROUPS == 0 and rows_per_block % 8 == 0
    token = pl.BlockSpec((tm, d), lambda i, e: (i, 0))
    part = pl.BlockSpec((PEER_HEADS, rows_per_block, tm), lambda i, e: (0, e, i))
    return pl.pallas_call(
        functools.partial(_peer_expert_kernel, tm=tm, rows_per_block=rows_per_block),
        out_shape=jax.ShapeDtypeStruct((n, d), F32),
        grid=(n // tm, n_blocks),
        in_specs=[
            token,
            pl.BlockSpec((None, te, d), lambda i, e: (layer, e, 0)),
            pl.BlockSpec((None, None, d, te), lambda i, e: (layer, e, 0, 0)),
            part, part,
            pl.BlockSpec((PEER_HEADS, N_KEYS, tm), lambda i, e: (0, 0, i)),
            token,
            pl.BlockSpec((1, 1, d), lambda i, e: (brow(i) * 6 + 5, 0, 0)),
        ],
        out_specs=token,
        scratch_shapes=[pltpu.VMEM((d, tm), F32), pltpu.VMEM((te, tm), F32), pltpu.VMEM((te, tm), BF16)],
        compiler_params=_params(("arbitrary", "arbitrary")),
        name="peer_experts",
    )(h2, u_tab, vt_tab, e1, thr, e2, xs, mod3)


def _rope_tables(batch, seq, n_ctx):
    rows = seq // GRID_W
    row = jnp.repeat(jnp.arange(rows), GRID_W).astype(F32)
    col = jnp.tile(jnp.arange(GRID_W), rows).astype(F32)
    half = ROPE_DIM // 2
    inv_freq = ROPE_BASE ** (-jnp.arange(0, half, 2, dtype=F32) / half)
    ang = jnp.concatenate([row[:, None] * inv_freq, col[:, None] * inv_freq], axis=-1)
    cos = jnp.concatenate([jnp.tile(jnp.cos(ang), (batch, 1)), jnp.ones((n_ctx, half), F32)], axis=0)
    sin = jnp.concatenate([jnp.tile(jnp.sin(ang), (batch, 1)), jnp.zeros((n_ctx, half), F32)], axis=0)
    zero = jnp.zeros_like(cos)
    c1 = jnp.concatenate([cos, cos, zero, zero], axis=-1)
    s1 = jnp.concatenate([zero, sin, zero, zero], axis=-1)
    s2 = jnp.concatenate([-sin, zero, zero, zero], axis=-1)
    return c1, s1, s2


def _deinterleave(w):
    return jnp.concatenate([w[..., 0::2], w[..., 1::2]], axis=-1)


def _head_gain(g):
    pad = jnp.zeros(g.shape[:-1] + (QK_PAD - QK_DIM,), F32)
    return jnp.concatenate([g[..., :NOPE_DIM], _deinterleave(g[..., NOPE_DIM:]), pad], axis=-1)[:, None, :]


def kernel(x, c, ctx, c_ctx, w_mod, b_mod, ln_mix, w_in, q_a_norm, kv_a_norm, w_q_up, w_kv_up, q_norm, k_norm,
           conv_w, w_attn_out, w_conv_out, w_o, ln_ffn, w_query, sub_keys, u_experts, v_experts):
    batch, seq, d = x.shape
    ctx_len = ctx.shape[1]
    depth = w_mod.shape[0]
    n_lat, n_ctx = batch * seq, batch * ctx_len
    n_exp = u_experts.shape[1]
    t = _tiles(seq, n_ctx)
    assert n_lat % ctx_len == 0 and seq % GRID_W == 0 and batch < MOD_ROWS and n_exp % t.te == 0

    blocks_per_seq = seq // t.tm
    brow = lambda i: jnp.minimum(i // blocks_per_seq, batch)

    xs = jnp.concatenate([x.reshape(n_lat, d), ctx.reshape(n_ctx, d)], axis=0)
    cc = jnp.concatenate([c, c_ctx[None, :], jnp.zeros((MOD_ROWS - batch - 1, d), F32)], axis=0)
    c1, s1, s2 = _rope_tables(batch, seq, n_ctx)

    pos = jnp.concatenate([jnp.tile(jnp.arange(seq), batch), jnp.tile(jnp.arange(ctx_len), batch)])
    length = jnp.concatenate([jnp.full((n_lat,), seq), jnp.full((n_ctx,), ctx_len)])
    has_prev = (pos != 0).astype(F32)[:, None]
    has_next = (pos != length - 1).astype(F32)[:, None]

    o_kv, o_kr, o_q = 0, KV_LORA, KV_LORA + ROPE_DIM
    o_b = o_q + Q_LORA
    tn = t.tn_in
    w_small = jnp.concatenate(
        [w_in[..., o_kv:o_kv + KV_LORA], w_in[..., o_q:o_q + Q_LORA], _deinterleave(w_in[..., o_kr:o_kr + ROPE_DIM]),
         jnp.zeros((depth, d, LANES - ROPE_DIM), F32)], axis=-1).astype(BF16)
    w_big = (w_in[..., o_b:].reshape(depth, d, 5, d // tn, tn).transpose(0, 1, 3, 2, 4)
             .reshape(depth, d, 5 * d).astype(BF16))
    wq = w_q_up.reshape(depth, Q_LORA, MLA_HEADS, QK_DIM)
    wq = jnp.concatenate([wq[..., :NOPE_DIM], _deinterleave(wq[..., NOPE_DIM:]),
                          jnp.zeros((depth, Q_LORA, MLA_HEADS, QK_PAD - QK_DIM), F32)], axis=-1)
    wq = wq.reshape(depth, Q_LORA, MLA_HEADS * QK_PAD).astype(BF16)
    wkv = w_kv_up.astype(BF16)
    wao, wco, wo = w_attn_out.astype(BF16), w_conv_out.astype(BF16), w_o.astype(BF16)
    wqry, sk = w_query.astype(BF16), sub_keys.astype(BF16)
    u_tab = u_experts.astype(BF16)
    vt_tab = v_experts.reshape(depth, n_exp // t.te, t.te, d).transpose(0, 1, 3, 2).astype(BF16)
    qg, kg = _head_gain(q_norm), _head_gain(k_norm)
    ln_mix3, ln_ffn3 = ln_mix[:, None, :], ln_ffn[:, None, :]
    qan3, kvan3 = q_a_norm[:, None, :], kv_a_norm[:, None, :]
    b_mod3 = b_mod[:, None, :]

    for l in range(depth):
        last = l == depth - 1
        mod3 = _ada_params(cc, w_mod, b_mod3, l).reshape(MOD_ROWS * 6, 1, d)

        ps, bg, z, ga, gc = _in_projection(xs, ln_mix3, mod3, w_small, w_big, l, t, brow)
        q, k, v = _qkv_heads(ps, qan3, kvan3, wq, wkv, qg, kg, c1, s1, s2, l, t)
        attn = _attention_latent(q, k, v, batch, seq, ctx_len, t)
        if not last:
            attn = jnp.concatenate([attn, _attention_context(q, k, v, batch, n_lat, ctx_len)], axis=0)
        m = _merge(attn, bg, z, has_prev, has_next, conv_w, wao, wco, ga, gc, l, t)
        xs = _out_projection(m, wo, xs, mod3, l, t, brow)

        h2, pe1, pthr, pe2 = _peer_select(xs, ln_ffn3, mod3, wqry, sk, l, t, brow)
        xs = _peer_experts(h2, u_tab, vt_tab, pe1, pthr, pe2, xs, mod3, l, t, brow)

    return xs[:n_lat].reshape(batch, seq, d)
```

```python
import functools
from typing import NamedTuple

import jax
import jax.numpy as jnp
from jax import lax
from jax.experimental import pallas as pl
from jax.experimental.pallas import tpu as pltpu

F32 = jnp.float32
BF16 = jnp.bfloat16

GRID_W = 64
MLA_HEADS = 16
Q_LORA = 512
KV_LORA = 256
NOPE_DIM = 128
ROPE_DIM = 64
V_DIM = 128
QK_DIM = NOPE_DIM + ROPE_DIM
QK_PAD = 256
ATTN_SCALE = QK_DIM ** -0.5
LOG2_E = 1.4426950408889634
ROPE_BASE = 10000.0
PEER_HEADS = 8
PEER_HALF = 128
N_KEYS = 128
PEER_TOPK = 16
EPS = 1e-6
MOD_ROWS = 8
LANES = 128
BF16_SUBLANES = 16
VMEM_LIMIT = 56 * 1024 * 1024
NEG_INF = float("-inf")
ROW_GROUPS = 4
VALUE_PIECES = 4
KEY_SPLIT = 4
ATTN_SUB = 256
ATTN_KEYS = 1024
SMALL_COLS = KV_LORA + Q_LORA + LANES


class Tiles(NamedTuple):
    tm: int
    tn_in: int
    tn_merge: int
    tn_out: int
    te: int
    tq: int


def _tiles(seq, n_ctx):
    tm = 512
    assert seq % tm == 0 and n_ctx % tm == 0
    return Tiles(tm=tm, tn_in=512, tn_merge=512, tn_out=1024, te=ROW_GROUPS * 2 * N_KEYS, tq=min(8 * ATTN_SUB, seq))


def _params(sem, flags=None):
    return pltpu.CompilerParams(dimension_semantics=sem, vmem_limit_bytes=VMEM_LIMIT, flags=flags)


def _nt_dot(a, b):
    return lax.dot_general(a, b, (((1,), (1,)), ((), ())), preferred_element_type=F32)


def _ada_kernel(c_ref, w_ref, b_ref, o_ref):
    a = c_ref[...]
    a = a / (1.0 + jnp.exp(-a))
    o_ref[...] = jnp.dot(a.astype(BF16), w_ref[...].astype(BF16), preferred_element_type=F32) + b_ref[...]


def _ada_params(cc, w_mod, b_mod, layer):
    rows, d = cc.shape
    cols = w_mod.shape[2]
    tn = 1024
    return pl.pallas_call(
        _ada_kernel,
        out_shape=jax.ShapeDtypeStruct((rows, cols), F32),
        grid=(cols // tn,),
        in_specs=[
            pl.BlockSpec((rows, d), lambda j: (0, 0)),
            pl.BlockSpec((None, d, tn), lambda j: (layer, 0, j)),
            pl.BlockSpec((None, 1, tn), lambda j: (layer, 0, j)),
        ],
        out_specs=pl.BlockSpec((rows, tn), lambda j: (0, j)),
        compiler_params=_params(("arbitrary",)),
        name="ada_params",
    )(cc, w_mod, b_mod)


def _norm_mod(x, gain, shift, scale):
    ms = jnp.mean(x * x, axis=-1, keepdims=True)
    y = x * lax.rsqrt(ms + EPS) * gain
    return y * (1.0 + scale) + shift


def _inproj_kernel(x_ref, ln_ref, sh_ref, sc_ref, ws_ref, wb_ref,
                   ps_ref, b_ref, z_ref, ga_ref, gc_ref, h_scr, *, tn):
    @pl.when(pl.program_id(1) == 0)
    def _():
        h = _norm_mod(x_ref[...], ln_ref[0], sh_ref[0], sc_ref[0]).astype(BF16)
        h_scr[...] = h
        ps_ref[...] = jnp.dot(h, ws_ref[...], preferred_element_type=F32)

    r = jnp.dot(h_scr[...], wb_ref[...], preferred_element_type=F32)
    b_ref[...] = r[:, 0:tn].astype(BF16)
    z_ref[...] = (r[:, tn:2 * tn] * r[:, 2 * tn:3 * tn]).astype(BF16)
    ga_ref[...] = (1.0 / (1.0 + jnp.exp(-r[:, 3 * tn:4 * tn]))).astype(BF16)
    gc_ref[...] = (1.0 / (1.0 + jnp.exp(-r[:, 4 * tn:5 * tn]))).astype(BF16)


def _in_projection(xs, ln, mod3, w_small, w_big, layer, t, brow):
    n, d = xs.shape
    tm, tn = t.tm, t.tn_in
    wide = pl.BlockSpec((tm, tn), lambda i, j: (i, j))
    out_wide = jax.ShapeDtypeStruct((n, d), BF16)
    return pl.pallas_call(
        functools.partial(_inproj_kernel, tn=tn),
        out_shape=(jax.ShapeDtypeStruct((n, SMALL_COLS), F32), out_wide, out_wide, out_wide, out_wide),
        grid=(n // tm, d // tn),
        in_specs=[
            pl.BlockSpec((tm, d), lambda i, j: (i, 0)),
            pl.BlockSpec((1, 1, d), lambda i, j: (layer, 0, 0)),
            pl.BlockSpec((1, 1, d), lambda i, j: (brow(i) * 6 + 0, 0, 0)),
            pl.BlockSpec((1, 1, d), lambda i, j: (brow(i) * 6 + 1, 0, 0)),
            pl.BlockSpec((None, d, SMALL_COLS), lambda i, j: (layer, 0, 0)),
            pl.BlockSpec((None, d, 5 * tn), lambda i, j: (layer, 0, j)),
        ],
        out_specs=(pl.BlockSpec((tm, SMALL_COLS), lambda i, j: (i, 0)), wide, wide, wide, wide),
        scratch_shapes=[pltpu.VMEM((tm, d), BF16)],
        compiler_params=_params(("arbitrary", "arbitrary")),
        name="in_projection",
    )(xs, ln, mod3, mod3, w_small, w_big)


def _rope(r, c1, s1, s2):
    return r * c1 + pltpu.roll(r, 32, axis=1) * s1 + pltpu.roll(r, 96, axis=1) * s2


def _rms(x, true_width):
    return lax.rsqrt(jnp.sum(x * x, axis=-1, keepdims=True) * (1.0 / true_width) + EPS)


def _qkv_kernel(ps_ref, qan_ref, kvan_ref, wq_ref, wkv_ref, qg_ref, kg_ref, c1_ref, s1_ref, s2_ref,
                q_ref, k_ref, v_ref):
    c1, s1, s2 = c1_ref[...], s1_ref[...], s2_ref[...]
    kv_lat = ps_ref[:, 0:KV_LORA]
    q_lat = ps_ref[:, KV_LORA:KV_LORA + Q_LORA]
    kr = ps_ref[:, KV_LORA + Q_LORA:SMALL_COLS]
    kvn = (kv_lat * _rms(kv_lat, KV_LORA) * kvan_ref[0]).astype(BF16)
    qn = (q_lat * _rms(q_lat, Q_LORA) * qan_ref[0]).astype(BF16)
    kr_ssq = jnp.sum(kr * kr, axis=-1, keepdims=True)
    qg, kg = qg_ref[0], kg_ref[0]
    kr_rot = _rope(kr * kg[:, NOPE_DIM:], c1, s1, s2)
    for h in range(MLA_HEADS):
        qh = jnp.dot(qn, wq_ref[:, h * QK_PAD:(h + 1) * QK_PAD], preferred_element_type=F32)
        qh = qh * _rms(qh, QK_DIM) * qg
        q_rot = _rope(qh[:, NOPE_DIM:], c1, s1, s2)
        q_ref[h] = (jnp.concatenate([qh[:, :NOPE_DIM], q_rot], axis=-1) * (ATTN_SCALE * LOG2_E)).astype(BF16)

        kvh = jnp.dot(kvn, wkv_ref[:, h * QK_PAD:(h + 1) * QK_PAD], preferred_element_type=F32)
        k_nope = kvh[:, :NOPE_DIM]
        ssq = jnp.sum(k_nope * k_nope, axis=-1, keepdims=True) + kr_ssq
        r = lax.rsqrt(ssq * (1.0 / QK_DIM) + EPS)
        k_ref[h] = (jnp.concatenate([k_nope * kg[:, :NOPE_DIM], kr_rot], axis=-1) * r).astype(BF16)
        v_ref[h] = kvh[:, NOPE_DIM:].astype(BF16)


def _qkv_heads(ps, qan, kvan, wq, wkv, qg, kg, c1, s1, s2, layer, t):
    n = ps.shape[0]
    tm = t.tm
    row = lambda i: (i, 0)
    per_layer = lambda i: (layer, 0, 0)
    out = lambda i: (0, i, 0)
    return pl.pallas_call(
        _qkv_kernel,
        out_shape=(jax.ShapeDtypeStruct((MLA_HEADS, n, QK_PAD), BF16),
                   jax.ShapeDtypeStruct((MLA_HEADS, n, QK_PAD), BF16),
                   jax.ShapeDtypeStruct((MLA_HEADS, n, V_DIM), BF16)),
        grid=(n // tm,),
        in_specs=[
            pl.BlockSpec((tm, SMALL_COLS), row),
            pl.BlockSpec((1, 1, Q_LORA), per_layer),
            pl.BlockSpec((1, 1, KV_LORA), per_layer),
            pl.BlockSpec((None, Q_LORA, MLA_HEADS * QK_PAD), per_layer),
            pl.BlockSpec((None, KV_LORA, MLA_HEADS * QK_PAD), per_layer),
            pl.BlockSpec((1, 1, QK_PAD), per_layer),
            pl.BlockSpec((1, 1, QK_PAD), per_layer),
            pl.BlockSpec((tm, LANES), row),
            pl.BlockSpec((tm, LANES), row),
            pl.BlockSpec((tm, LANES), row),
        ],
        out_specs=(pl.BlockSpec((MLA_HEADS, tm, QK_PAD), out),
                   pl.BlockSpec((MLA_HEADS, tm, QK_PAD), out),
                   pl.BlockSpec((MLA_HEADS, tm, V_DIM), out)),
        compiler_params=_params(("arbitrary",)),
        name="qkv_heads",
    )(ps, qan, kvan, wq, wkv, qg, kg, c1, s1, s2)


def _attn_lat_kernel(q_ref, kl_ref, kc_ref, vl_ref, vc_ref, o_ref, s_scr, m_scr, *, tq, seq):
    n_sub = tq // ATTN_SUB
    keys = min(ATTN_KEYS, seq)
    lat_pieces = [(kl_ref, vl_ref, p * keys, keys) for p in range(seq // keys)]
    pieces = lat_pieces + [(kc_ref, vc_ref, 0, kc_ref.shape[0])]
    offsets = [p * keys for p in range(len(lat_pieces))] + [seq]

    def scores(sb):
        slot = sb % 2
        q = q_ref[sb * ATTN_SUB:(sb + 1) * ATTN_SUB, :]
        state = {"m": None}

        def piece(idx):
            def run():
                k_ref, _, start, size = pieces[idx]
                s = _nt_dot(q, k_ref[start:start + size, :])
                s_scr[slot, :, offsets[idx]:offsets[idx] + size] = s
                row_max = jnp.max(s, axis=-1, keepdims=True)
                state["m"] = row_max if state["m"] is None else jnp.maximum(state["m"], row_max)
                if idx == len(pieces) - 1:
                    m_scr[slot] = state["m"]
            return run
        return [piece(idx) for idx in range(len(pieces))]

    def outputs(sb):
        slot = sb % 2
        state = {"acc": None, "den": None}

        def piece(idx):
            def run():
                _, v_ref, start, size = pieces[idx]
                p = jnp.exp2(s_scr[slot, :, offsets[idx]:offsets[idx] + size] - m_scr[slot])
                den = jnp.sum(p, axis=-1, keepdims=True)
                acc = jnp.dot(p.astype(BF16), v_ref[start:start + size, :], preferred_element_type=F32)
                state["den"] = den if state["den"] is None else state["den"] + den
                state["acc"] = acc if state["acc"] is None else state["acc"] + acc
                if idx == len(pieces) - 1:
                    o_ref[sb * ATTN_SUB:(sb + 1) * ATTN_SUB, :] = (state["acc"] / state["den"]).astype(BF16)
            return run
        return [piece(idx) for idx in range(len(pieces))]

    for run in scores(0):
        run()
    for sb in range(n_sub):
        nxt = scores(sb + 1) if sb + 1 < n_sub else []
        cur = outputs(sb)
        for idx in range(len(pieces)):
            if nxt:
                nxt[idx]()
            cur[idx]()


def _attn_ctx_kernel(q_ref, kc_ref, vc_ref, o_ref):
    s_c = _nt_dot(q_ref[...], kc_ref[...])
    p_c = jnp.exp2(s_c - jnp.max(s_c, axis=-1, keepdims=True))
    o = jnp.dot(p_c.astype(BF16), vc_ref[...], preferred_element_type=F32)
    o_ref[...] = (o / jnp.sum(p_c, axis=-1, keepdims=True)).astype(BF16)


def _attention_latent(q, k, v, batch, seq, ctx_len, t):
    n_lat = batch * seq
    tq = t.tq
    nq = seq // tq
    ctx0 = n_lat // ctx_len
    assert tq % ATTN_SUB == 0 and seq % min(ATTN_KEYS, seq) == 0
    return pl.pallas_call(
        functools.partial(_attn_lat_kernel, tq=tq, seq=seq),
        out_shape=jax.ShapeDtypeStruct((n_lat, MLA_HEADS * V_DIM), BF16),
        grid=(batch, MLA_HEADS, nq),
        in_specs=[
            pl.BlockSpec((None, tq, QK_PAD), lambda b, h, i: (h, b * nq + i, 0)),
            pl.BlockSpec((None, seq, QK_PAD), lambda b, h, i: (h, b, 0)),
            pl.BlockSpec((None, ctx_len, QK_PAD), lambda b, h, i: (h, ctx0 + b, 0)),
            pl.BlockSpec((None, seq, V_DIM), lambda b, h, i: (h, b, 0)),
            pl.BlockSpec((None, ctx_len, V_DIM), lambda b, h, i: (h, ctx0 + b, 0)),
        ],
        out_specs=pl.BlockSpec((tq, V_DIM), lambda b, h, i: (b * nq + i, h)),
        scratch_shapes=[pltpu.VMEM((2, ATTN_SUB, seq + ctx_len), F32), pltpu.VMEM((2, ATTN_SUB, 1), F32)],
        compiler_params=_params(("arbitrary", "arbitrary", "arbitrary")),
        name="attention_latent",
    )(q, k, k, v, v)


def _attention_context(q, k, v, batch, n_lat, ctx_len):
    ctx0 = n_lat // ctx_len
    blk = lambda b, h: (h, ctx0 + b, 0)
    return pl.pallas_call(
        _attn_ctx_kernel,
        out_shape=jax.ShapeDtypeStruct((batch * ctx_len, MLA_HEADS * V_DIM), BF16),
        grid=(batch, MLA_HEADS),
        in_specs=[
            pl.BlockSpec((None, ctx_len, QK_PAD), blk),
            pl.BlockSpec((None, ctx_len, QK_PAD), blk),
            pl.BlockSpec((None, ctx_len, V_DIM), blk),
        ],
        out_specs=pl.BlockSpec((ctx_len, V_DIM), lambda b, h: (b, h)),
        compiler_params=_params(("arbitrary", "arbitrary")),
        name="attention_context",
    )(q, k, v)


def _merge_kernel(attn_ref, bg_ref, z_ref, zp_ref, zn_ref, hp_ref, hn_ref, cw_ref, wao_ref, wco_ref,
                  ga_ref, gc_ref, m_ref, yc_scr, *, tm):
    @pl.when(pl.program_id(1) == 0)
    def _():
        z = z_ref[...].astype(F32)
        rows = lax.broadcasted_iota(jnp.int32, (tm, 1), 0)
        z_prev = jnp.where(rows == 0, zp_ref[BF16_SUBLANES - 1:BF16_SUBLANES, :].astype(F32),
                           pltpu.roll(z, 1, axis=0)) * hp_ref[...]
        z_next = jnp.where(rows == tm - 1, zn_ref[0:1, :].astype(F32),
                           pltpu.roll(z, tm - 1, axis=0)) * hn_ref[...]
        y = cw_ref[0:1, :] * z_prev + cw_ref[1:2, :] * z + cw_ref[2:3, :] * z_next
        yc_scr[...] = (bg_ref[...].astype(F32) * y).astype(BF16)

    y_attn = jnp.dot(attn_ref[...], wao_ref[...], preferred_element_type=F32)
    y_conv = jnp.dot(yc_scr[...], wco_ref[...], preferred_element_type=F32)
    m_ref[...] = (ga_ref[...].astype(F32) * y_attn + gc_ref[...].astype(F32) * y_conv).astype(BF16)


def _merge(attn, bg, z, has_prev, has_next, conv_w, wao, wco, ga, gc, layer, t):
    n, d = attn.shape
    tm, tn = t.tm, t.tn_merge
    halo = tm // BF16_SUBLANES
    last_halo = z.shape[0] // BF16_SUBLANES - 1
    row = lambda i, j: (i, 0)
    return pl.pallas_call(
        functools.partial(_merge_kernel, tm=tm),
        out_shape=jax.ShapeDtypeStruct((n, d), BF16),
        grid=(n // tm, d // tn),
        in_specs=[
            pl.BlockSpec((tm, d), row),
            pl.BlockSpec((tm, d), row),
            pl.BlockSpec((tm, d), row),
            pl.BlockSpec((BF16_SUBLANES, d), lambda i, j: (jnp.maximum(i * halo - 1, 0), 0)),
            pl.BlockSpec((BF16_SUBLANES, d), lambda i, j: (jnp.minimum((i + 1) * halo, last_halo), 0)),
            pl.BlockSpec((tm, 1), row),
            pl.BlockSpec((tm, 1), row),
            pl.BlockSpec((None, 3, d), lambda i, j: (layer, 0, 0)),
            pl.BlockSpec((None, d, tn), lambda i, j: (layer, 0, j)),
            pl.BlockSpec((None, d, tn), lambda i, j: (layer, 0, j)),
            pl.BlockSpec((tm, tn), lambda i, j: (i, j)),
            pl.BlockSpec((tm, tn), lambda i, j: (i, j)),
        ],
        out_specs=pl.BlockSpec((tm, tn), lambda i, j: (i, j)),
        scratch_shapes=[pltpu.VMEM((tm, d), BF16)],
        compiler_params=_params(("arbitrary", "arbitrary")),
        name="merge_branches",
    )(attn, bg, z, z, z, has_prev, has_next, conv_w, wao, wco, ga, gc)


def _outproj_kernel(m_ref, wo_ref, x_ref, g_ref, o_ref):
    y = jnp.dot(m_ref[...], wo_ref[...], preferred_element_type=F32)
    o_ref[...] = x_ref[...] + g_ref[0] * y


def _out_projection(m, wo, xs, mod3, layer, t, brow):
    n, d = m.shape
    tm, tn = t.tm, t.tn_out
    return pl.pallas_call(
        _outproj_kernel,
        out_shape=jax.ShapeDtypeStruct((n, d), F32),
        grid=(n // tm, d // tn),
        in_specs=[
            pl.BlockSpec((tm, d), lambda i, j: (i, 0)),
            pl.BlockSpec((None, d, tn), lambda i, j: (layer, 0, j)),
            pl.BlockSpec((tm, tn), lambda i, j: (i, j)),
            pl.BlockSpec((1, 1, tn), lambda i, j: (brow(i) * 6 + 2, 0, j)),
        ],
        out_specs=pl.BlockSpec((tm, tn), lambda i, j: (i, j)),
        compiler_params=_params(("arbitrary", "arbitrary")),
        name="out_projection",
    )(m, wo, xs, mod3)


def _merge_exchange_pairs(n):
    pairs = []
    t = max(1, (n - 1).bit_length())
    p = 1 << (t - 1)
    while p > 0:
        q, r, d = 1 << (t - 1), 0, p
        while d > 0:
            pairs += [(i, i + d) for i in range(n - d) if (i & p) == r]
            d, q, r = q - p, q >> 1, p
        p >>= 1
    return pairs


def _top_values(s, count):
    groups = [s[g * 8:(g + 1) * 8] for g in range(s.shape[0] // 8)]
    for i, j in _merge_exchange_pairs(len(groups)):
        groups[i], groups[j] = jnp.maximum(groups[i], groups[j]), jnp.minimum(groups[i], groups[j])
    groups = groups[:count] + [jnp.full_like(groups[0], NEG_INF)] * max(0, count - len(groups))
    vals = []
    for t in range(count):
        m = jnp.max(groups[0], axis=0, keepdims=True)
        vals.append(m)
        taken = groups[0] == m
        for k in range(count - 1 - t):
            groups[k] = jnp.where(taken, groups[k + 1], groups[k])
    return vals


def _peer_select_kernel(x_ref, ln_ref, sh_ref, sc_ref, wq_ref, sk_ref, h_ref, e1_ref, thr_ref, e2_ref):
    @pl.when(pl.program_id(1) == 0)
    def _():
        h_ref[...] = _norm_mod(x_ref[...], ln_ref[0], sh_ref[0], sc_ref[0]).astype(BF16)

    qp = jnp.dot(h_ref[...], wq_ref[...], preferred_element_type=F32)
    s1_all = _nt_dot(sk_ref[0], qp[:, :PEER_HALF].astype(BF16))
    s2_all = _nt_dot(sk_ref[1], qp[:, PEER_HALF:].astype(BF16))
    half = PEER_TOPK // 2
    never = 2.0
    for c in range(s1_all.shape[1] // LANES):
        cols = slice(c * LANES, (c + 1) * LANES)
        s1, s2 = s1_all[:, cols], s2_all[:, cols]
        top1 = _top_values(s1, PEER_TOPK)
        top2 = _top_values(s2, PEER_TOPK)
        col1 = jnp.concatenate(top1, axis=0)
        col2 = jnp.concatenate(top2, axis=0)
        cand = jnp.concatenate([top1[0] + col2] + [top1[k] + col2[:half] for k in range(1, half)]
                               + [col1[half:] + top2[0]], axis=0)
        best = _top_values(cand, PEER_TOPK)
        tau = best[PEER_TOPK - 1]
        norm = jnp.zeros_like(tau)
        for b in best:
            norm = norm + jnp.exp(b - best[0])
        e2_top = jnp.exp(col2 - top2[0])
        thr = jnp.full(s1.shape, never, F32)
        for k in range(PEER_TOPK):
            thr_k = jnp.min(jnp.where(top1[k] + col2 >= tau, e2_top, never), axis=0, keepdims=True)
            thr = jnp.where(s1 == top1[k], thr_k, thr)
        e1_ref[:, cols] = jnp.exp(s1 - top1[0]) * (0.5 / norm)
        thr_ref[:, cols] = thr
        e2_ref[:, cols] = jnp.exp(s2 - top2[0])


def _peer_select(xs, ln, mod3, wq, sk, layer, t, brow):
    n, d = xs.shape
    tm = t.tm
    score = jax.ShapeDtypeStruct((PEER_HEADS, N_KEYS, n), F32)
    score_spec = pl.BlockSpec((None, N_KEYS, tm), lambda i, h: (h, 0, i))
    return pl.pallas_call(
        _peer_select_kernel,
        out_shape=(jax.ShapeDtypeStruct((n, d), BF16), score, score, score),
        grid=(n // tm, PEER_HEADS),
        in_specs=[
            pl.BlockSpec((tm, d), lambda i, h: (i, 0)),
            pl.BlockSpec((1, 1, d), lambda i, h: (layer, 0, 0)),
            pl.BlockSpec((1, 1, d), lambda i, h: (brow(i) * 6 + 3, 0, 0)),
            pl.BlockSpec((1, 1, d), lambda i, h: (brow(i) * 6 + 4, 0, 0)),
            pl.BlockSpec((None, d, 2 * PEER_HALF), lambda i, h: (layer, 0, h)),
            pl.BlockSpec((None, None, 2, N_KEYS, PEER_HALF), lambda i, h: (layer, h, 0, 0, 0)),
        ],
        out_specs=(pl.BlockSpec((tm, d), lambda i, h: (i, 0)), score_spec, score_spec, score_spec),
        compiler_params=_params(("arbitrary", "arbitrary")),
        name="peer_select",
    )(xs, ln, mod3, mod3, wq, sk)


def _peer_expert_kernel(h_ref, u_ref, vt_ref, e1_ref, thr_ref, e2_ref, x_ref, g_ref,
                        o_ref, acc_scr, a_scr, p_scr, *, tm, rows_per_block):
    e = pl.program_id(1)

    @pl.when(e == 0)
    def _():
        acc_scr[...] = jnp.zeros_like(acc_scr)

    d = acc_scr.shape[0]
    group = rows_per_block // ROW_GROUPS
    sub = group * N_KEYS
    jh = N_KEYS // KEY_SPLIT
    d_piece = d // VALUE_PIECES

    def hidden(q):
        def run():
            experts = slice(q * sub, (q + 1) * sub)
            a_scr[experts, :] = _nt_dot(u_ref[experts, :], h_ref[...])
        return run

    def values(q, r):
        def run():
            experts = slice(q * sub, (q + 1) * sub)
            rows = slice(r * d_piece, (r + 1) * d_piece)
            acc_scr[rows, :] += jnp.dot(vt_ref[rows, experts], p_scr[experts, :], preferred_element_type=F32)
        return run

    def weights_tile(q, tc, half):
        def run():
            cols = slice(tc * LANES, (tc + 1) * LANES)
            keys = slice(half * jh, (half + 1) * jh)
            weights = [jnp.zeros((jh, LANES), F32) for _ in range(group)]
            for hd in range(PEER_HEADS):
                e2_t = e2_ref[hd, keys, cols]
                for k in range(group):
                    ii = q * group + k
                    picked = e2_t >= thr_ref[hd, ii:ii + 1, cols]
                    weights[k] = weights[k] + jnp.where(picked, e2_t * e1_ref[hd, ii:ii + 1, cols], 0.0)
            for k in range(group):
                r0 = (q * group + k) * N_KEYS + half * jh
                rows = slice(r0, r0 + jh)
                a = a_scr[rows, cols]
                p_scr[rows, cols] = (weights[k] * (a + a * lax.erf(a * (2.0 ** -0.5)))).astype(BF16)
        return run

    hidden(0)()
    for q in range(ROW_GROUPS):
        vpu = [weights_tile(q, tc, half) for tc in range(tm // LANES) for half in range(KEY_SPLIT)]
        mxu = []
        if q + 1 < ROW_GROUPS:
            mxu.append(hidden(q + 1))
        if q > 0:
            mxu += [values(q - 1, r) for r in range(VALUE_PIECES)]
        for k, tile in enumerate(vpu):
            tile()
            for piece in mxu[k * len(mxu) // len(vpu):(k + 1) * len(mxu) // len(vpu)]:
                piece()
    for r in range(VALUE_PIECES):
        values(ROW_GROUPS - 1, r)()

    @pl.when(e == pl.num_programs(1) - 1)
    def _():
        o_ref[...] = x_ref[...] + g_ref[0] * acc_scr[...].T


def _peer_experts(h2, u_tab, vt_tab, e1, thr, e2, xs, mod3, layer, t, brow):
    n, d = xs.shape
    tm, te = t.tm, t.te
    n_blocks = vt_tab.shape[1]
    rows_per_block = te // N_KEYS
    assert rows_per_block % ROW_GROUPS == 0 and rows_per_block % 8 == 0
    token = pl.BlockSpec((tm, d), lambda i, e: (i, 0))
    part = pl.BlockSpec((PEER_HEADS, rows_per_block, tm), lambda i, e: (0, e, i))
    return pl.pallas_call(
        functools.partial(_peer_expert_kernel, tm=tm, rows_per_block=rows_per_block),
        out_shape=jax.ShapeDtypeStruct((n, d), F32),
        grid=(n // tm, n_blocks),
        in_specs=[
            token,
            pl.BlockSpec((None, te, d), lambda i, e: (layer, e, 0)),
            pl.BlockSpec((None, None, d, te), lambda i, e: (layer, e, 0, 0)),
            part, part,
            pl.BlockSpec((PEER_HEADS, N_KEYS, tm), lambda i, e: (0, 0, i)),
            token,
            pl.BlockSpec((1, 1, d), lambda i, e: (brow(i) * 6 + 5, 0, 0)),
        ],
        out_specs=token,
        scratch_shapes=[pltpu.VMEM((d, tm), F32), pltpu.VMEM((te, tm), F32), pltpu.VMEM((te, tm), BF16)],
        compiler_params=_params(("arbitrary", "arbitrary")),
        name="peer_experts",
    )(h2, u_tab, vt_tab, e1, thr, e2, xs, mod3)


def _rope_tables(batch, seq, n_ctx):
    rows = seq // GRID_W
    row = jnp.repeat(jnp.arange(rows), GRID_W).astype(F32)
    col = jnp.tile(jnp.arange(GRID_W), rows).astype(F32)
    half = ROPE_DIM // 2
    inv_freq = ROPE_BASE ** (-jnp.arange(0, half, 2, dtype=F32) / half)
    ang = jnp.concatenate([row[:, None] * inv_freq, col[:, None] * inv_freq], axis=-1)
    cos = jnp.concatenate([jnp.tile(jnp.cos(ang), (batch, 1)), jnp.ones((n_ctx, half), F32)], axis=0)
    sin = jnp.concatenate([jnp.tile(jnp.sin(ang), (batch, 1)), jnp.zeros((n_ctx, half), F32)], axis=0)
    zero = jnp.zeros_like(cos)
    c1 = jnp.concatenate([cos, cos, zero, zero], axis=-1)
    s1 = jnp.concatenate([zero, sin, zero, zero], axis=-1)
    s2 = jnp.concatenate([-sin, zero, zero, zero], axis=-1)
    return c1, s1, s2


def _deinterleave(w):
    return jnp.concatenate([w[..., 0::2], w[..., 1::2]], axis=-1)


def _head_gain(g):
    pad = jnp.zeros(g.shape[:-1] + (QK_PAD - QK_DIM,), F32)
    return jnp.concatenate([g[..., :NOPE_DIM], _deinterleave(g[..., NOPE_DIM:]), pad], axis=-1)[:, None, :]


def kernel(x, c, ctx, c_ctx, w_mod, b_mod, ln_mix, w_in, q_a_norm, kv_a_norm, w_q_up, w_kv_up, q_norm, k_norm,
           conv_w, w_attn_out, w_conv_out, w_o, ln_ffn, w_query, sub_keys, u_experts, v_experts):
    batch, seq, d = x.shape
    ctx_len = ctx.shape[1]
    depth = w_mod.shape[0]
    n_lat, n_ctx = batch * seq, batch * ctx_len
    n_exp = u_experts.shape[1]
    t = _tiles(seq, n_ctx)
    assert n_lat % ctx_len == 0 and seq % GRID_W == 0 and batch < MOD_ROWS and n_exp % t.te == 0

    blocks_per_seq = seq // t.tm
    brow = lambda i: jnp.minimum(i // blocks_per_seq, batch)

    xs = jnp.concatenate([x.reshape(n_lat, d), ctx.reshape(n_ctx, d)], axis=0)
    cc = jnp.concatenate([c, c_ctx[None, :], jnp.zeros((MOD_ROWS - batch - 1, d), F32)], axis=0)
    c1, s1, s2 = _rope_tables(batch, seq, n_ctx)

    pos = jnp.concatenate([jnp.tile(jnp.arange(seq), batch), jnp.tile(jnp.arange(ctx_len), batch)])
    length = jnp.concatenate([jnp.full((n_lat,), seq), jnp.full((n_ctx,), ctx_len)])
    has_prev = (pos != 0).astype(F32)[:, None]
    has_next = (pos != length - 1).astype(F32)[:, None]

    o_kv, o_kr, o_q = 0, KV_LORA, KV_LORA + ROPE_DIM
    o_b = o_q + Q_LORA
    tn = t.tn_in
    w_small = jnp.concatenate(
        [w_in[..., o_kv:o_kv + KV_LORA], w_in[..., o_q:o_q + Q_LORA], _deinterleave(w_in[..., o_kr:o_kr + ROPE_DIM]),
         jnp.zeros((depth, d, LANES - ROPE_DIM), F32)], axis=-1).astype(BF16)
    w_big = (w_in[..., o_b:].reshape(depth, d, 5, d // tn, tn).transpose(0, 1, 3, 2, 4)
             .reshape(depth, d, 5 * d).astype(BF16))
    wq = w_q_up.reshape(depth, Q_LORA, MLA_HEADS, QK_DIM)
    wq = jnp.concatenate([wq[..., :NOPE_DIM], _deinterleave(wq[..., NOPE_DIM:]),
                          jnp.zeros((depth, Q_LORA, MLA_HEADS, QK_PAD - QK_DIM), F32)], axis=-1)
    wq = wq.reshape(depth, Q_LORA, MLA_HEADS * QK_PAD).astype(BF16)
    wkv = w_kv_up.astype(BF16)
    wao, wco, wo = w_attn_out.astype(BF16), w_conv_out.astype(BF16), w_o.astype(BF16)
    wqry, sk = w_query.astype(BF16), sub_keys.astype(BF16)
    u_tab = u_experts.astype(BF16)
    vt_tab = v_experts.reshape(depth, n_exp // t.te, t.te, d).transpose(0, 1, 3, 2).astype(BF16)
    qg, kg = _head_gain(q_norm), _head_gain(k_norm)
    ln_mix3, ln_ffn3 = ln_mix[:, None, :], ln_ffn[:, None, :]
    qan3, kvan3 = q_a_norm[:, None, :], kv_a_norm[:, None, :]
    b_mod3 = b_mod[:, None, :]

    for l in range(depth):
        last = l == depth - 1
        mod3 = _ada_params(cc, w_mod, b_mod3, l).reshape(MOD_ROWS * 6, 1, d)

        ps, bg, z, ga, gc = _in_projection(xs, ln_mix3, mod3, w_small, w_big, l, t, brow)
        q, k, v = _qkv_heads(ps, qan3, kvan3, wq, wkv, qg, kg, c1, s1, s2, l, t)
        attn = _attention_latent(q, k, v, batch, seq, ctx_len, t)
        if not last:
            attn = jnp.concatenate([attn, _attention_context(q, k, v, batch, n_lat, ctx_len)], axis=0)
        m = _merge(attn, bg, z, has_prev, has_next, conv_w, wao, wco, ga, gc, l, t)
        xs = _out_projection(m, wo, xs, mod3, l, t, brow)

        h2, pe1, pthr, pe2 = _peer_select(xs, ln_ffn3, mod3, wqry, sk, l, t, brow)
        xs = _peer_experts(h2, u_tab, vt_tab, pe1, pthr, pe2, xs, mod3, l, t, brow)

    return xs[:n_lat].reshape(batch, seq, d)
```

```python
import functools
from typing import NamedTuple

import jax
import jax.numpy as jnp
from jax import lax
from jax.experimental import pallas as pl
from jax.experimental.pallas import tpu as pltpu

F32 = jnp.float32
BF16 = jnp.bfloat16

GRID_W = 64
MLA_HEADS = 16
Q_LORA = 512
KV_LORA = 256
NOPE_DIM = 128
ROPE_DIM = 64
V_DIM = 128
QK_DIM = NOPE_DIM + ROPE_DIM
QK_PAD = 256
ATTN_SCALE = QK_DIM ** -0.5
LOG2_E = 1.4426950408889634
ROPE_BASE = 10000.0
PEER_HEADS = 8
PEER_HALF = 128
N_KEYS = 128
PEER_TOPK = 16
EPS = 1e-6
MOD_ROWS = 8
LANES = 128
BF16_SUBLANES = 16
VMEM_LIMIT = 56 * 1024 * 1024
NEG_INF = float("-inf")
ROW_GROUPS = 4
VALUE_PIECES = 4
KEY_SPLIT = 4
ATTN_SUB = 256
ATTN_KEYS = 1024
SMALL_COLS = KV_LORA + Q_LORA + LANES


class Tiles(NamedTuple):
    tm: int
    tn_in: int
    tn_merge: int
    tn_out: int
    te: int
    tq: int


def _tiles(seq, n_ctx):
    tm = 512
    assert seq % tm == 0 and n_ctx % tm == 0
    return Tiles(tm=tm, tn_in=512, tn_merge=512, tn_out=1024, te=ROW_GROUPS * 2 * N_KEYS, tq=min(8 * ATTN_SUB, seq))


def _params(sem, flags=None):
    return pltpu.CompilerParams(dimension_semantics=sem, vmem_limit_bytes=VMEM_LIMIT, flags=flags)


def _nt_dot(a, b):
    return lax.dot_general(a, b, (((1,), (1,)), ((), ())), preferred_element_type=F32)


def _ada_kernel(c_ref, w_ref, b_ref, o_ref):
    a = c_ref[...]
    a = a / (1.0 + jnp.exp(-a))
    o_ref[...] = jnp.dot(a.astype(BF16), w_ref[...].astype(BF16), preferred_element_type=F32) + b_ref[...]


def _ada_params(cc, w_mod, b_mod, layer):
    rows, d = cc.shape
    cols = w_mod.shape[2]
    tn = 1024
    return pl.pallas_call(
        _ada_kernel,
        out_shape=jax.ShapeDtypeStruct((rows, cols), F32),
        grid=(cols // tn,),
        in_specs=[
            pl.BlockSpec((rows, d), lambda j: (0, 0)),
            pl.BlockSpec((None, d, tn), lambda j: (layer, 0, j)),
            pl.BlockSpec((None, 1, tn), lambda j: (layer, 0, j)),
        ],
        out_specs=pl.BlockSpec((rows, tn), lambda j: (0, j)),
        compiler_params=_params(("arbitrary",)),
        name="ada_params",
    )(cc, w_mod, b_mod)


def _norm_mod(x, gain, shift, scale):
    ms = jnp.mean(x * x, axis=-1, keepdims=True)
    y = x * lax.rsqrt(ms + EPS) * gain
    return y * (1.0 + scale) + shift


def _inproj_kernel(x_ref, ln_ref, sh_ref, sc_ref, ws_ref, wb_ref, wc_ref, wu_ref, wga_ref, wgc_ref,
                   ps_ref, b_ref, z_ref, ga_ref, gc_ref, h_scr):
    @pl.when(pl.program_id(1) == 0)
    def _():
        h = _norm_mod(x_ref[...], ln_ref[0], sh_ref[0], sc_ref[0]).astype(BF16)
        h_scr[...] = h
        ps_ref[...] = jnp.dot(h, ws_ref[...], preferred_element_type=F32)

    h = h_scr[...]
    proj = lambda w_ref: jnp.dot(h, w_ref[...], preferred_element_type=F32)
    b_ref[...] = proj(wb_ref).astype(BF16)
    z_ref[...] = (proj(wc_ref) * proj(wu_ref)).astype(BF16)
    ga_ref[...] = (1.0 / (1.0 + jnp.exp(-proj(wga_ref)))).astype(BF16)
    gc_ref[...] = (1.0 / (1.0 + jnp.exp(-proj(wgc_ref)))).astype(BF16)


def _in_projection(xs, ln, mod3, w_small, w_wide, layer, t, brow):
    n, d = xs.shape
    tm, tn = t.tm, t.tn_in
    wide = pl.BlockSpec((tm, tn), lambda i, j: (i, j))
    out_wide = jax.ShapeDtypeStruct((n, d), BF16)
    segment = lambda s: pl.BlockSpec((None, d, tn), lambda i, j: (layer, 0, s * (d // tn) + j))
    return pl.pallas_call(
        _inproj_kernel,
        out_shape=(jax.ShapeDtypeStruct((n, SMALL_COLS), F32), out_wide, out_wide, out_wide, out_wide),
        grid=(n // tm, d // tn),
        in_specs=[
            pl.BlockSpec((tm, d), lambda i, j: (i, 0)),
            pl.BlockSpec((1, 1, d), lambda i, j: (layer, 0, 0)),
            pl.BlockSpec((1, 1, d), lambda i, j: (brow(i) * 6 + 0, 0, 0)),
            pl.BlockSpec((1, 1, d), lambda i, j: (brow(i) * 6 + 1, 0, 0)),
            pl.BlockSpec((None, d, SMALL_COLS), lambda i, j: (layer, 0, 0)),
            segment(0), segment(1), segment(2), segment(3), segment(4),
        ],
        out_specs=(pl.BlockSpec((tm, SMALL_COLS), lambda i, j: (i, 0)), wide, wide, wide, wide),
        scratch_shapes=[pltpu.VMEM((tm, d), BF16)],
        compiler_params=_params(("arbitrary", "arbitrary")),
        name="in_projection",
    )(xs, ln, mod3, mod3, w_small, w_wide, w_wide, w_wide, w_wide, w_wide)


def _rope(r, c1, s1, s2):
    return r * c1 + pltpu.roll(r, 32, axis=1) * s1 + pltpu.roll(r, 96, axis=1) * s2


def _rms(x, true_width):
    return lax.rsqrt(jnp.sum(x * x, axis=-1, keepdims=True) * (1.0 / true_width) + EPS)


def _qkv_kernel(ps_ref, qan_ref, kvan_ref, wq_ref, wkv_ref, qg_ref, kg_ref, c1_ref, s1_ref, s2_ref,
                q_ref, k_ref, v_ref):
    c1, s1, s2 = c1_ref[...], s1_ref[...], s2_ref[...]
    kv_lat = ps_ref[:, 0:KV_LORA]
    q_lat = ps_ref[:, KV_LORA:KV_LORA + Q_LORA]
    kr = ps_ref[:, KV_LORA + Q_LORA:SMALL_COLS]
    kvn = (kv_lat * _rms(kv_lat, KV_LORA) * kvan_ref[0]).astype(BF16)
    qn = (q_lat * _rms(q_lat, Q_LORA) * qan_ref[0]).astype(BF16)
    kr_ssq = jnp.sum(kr * kr, axis=-1, keepdims=True)
    qg, kg = qg_ref[0], kg_ref[0]
    kr_rot = _rope(kr * kg[:, NOPE_DIM:], c1, s1, s2)
    for h in range(MLA_HEADS):
        qh = jnp.dot(qn, wq_ref[:, h * QK_PAD:(h + 1) * QK_PAD], preferred_element_type=F32)
        qh = qh * _rms(qh, QK_DIM) * qg
        q_rot = _rope(qh[:, NOPE_DIM:], c1, s1, s2)
        q_ref[h] = (jnp.concatenate([qh[:, :NOPE_DIM], q_rot], axis=-1) * (ATTN_SCALE * LOG2_E)).astype(BF16)

        kvh = jnp.dot(kvn, wkv_ref[:, h * QK_PAD:(h + 1) * QK_PAD], preferred_element_type=F32)
        k_nope = kvh[:, :NOPE_DIM]
        ssq = jnp.sum(k_nope * k_nope, axis=-1, keepdims=True) + kr_ssq
        r = lax.rsqrt(ssq * (1.0 / QK_DIM) + EPS)
        k_ref[h] = (jnp.concatenate([k_nope * kg[:, :NOPE_DIM], kr_rot], axis=-1) * r).astype(BF16)
        v_ref[h] = kvh[:, NOPE_DIM:].astype(BF16)


def _qkv_heads(ps, qan, kvan, wq, wkv, qg, kg, c1, s1, s2, layer, t):
    n = ps.shape[0]
    tm = t.tm
    row = lambda i: (i, 0)
    per_layer = lambda i: (layer, 0, 0)
    out = lambda i: (0, i, 0)
    return pl.pallas_call(
        _qkv_kernel,
        out_shape=(jax.ShapeDtypeStruct((MLA_HEADS, n, QK_PAD), BF16),
                   jax.ShapeDtypeStruct((MLA_HEADS, n, QK_PAD), BF16),
                   jax.ShapeDtypeStruct((MLA_HEADS, n, V_DIM), BF16)),
        grid=(n // tm,),
        in_specs=[
            pl.BlockSpec((tm, SMALL_COLS), row),
            pl.BlockSpec((1, 1, Q_LORA), per_layer),
            pl.BlockSpec((1, 1, KV_LORA), per_layer),
            pl.BlockSpec((None, Q_LORA, MLA_HEADS * QK_PAD), per_layer),
            pl.BlockSpec((None, KV_LORA, MLA_HEADS * QK_PAD), per_layer),
            pl.BlockSpec((1, 1, QK_PAD), per_layer),
            pl.BlockSpec((1, 1, QK_PAD), per_layer),
            pl.BlockSpec((tm, LANES), row),
            pl.BlockSpec((tm, LANES), row),
            pl.BlockSpec((tm, LANES), row),
        ],
        out_specs=(pl.BlockSpec((MLA_HEADS, tm, QK_PAD), out),
                   pl.BlockSpec((MLA_HEADS, tm, QK_PAD), out),
                   pl.BlockSpec((MLA_HEADS, tm, V_DIM), out)),
        compiler_params=_params(("arbitrary",)),
        name="qkv_heads",
    )(ps, qan, kvan, wq, wkv, qg, kg, c1, s1, s2)


def _attn_lat_kernel(q_ref, kl_ref, kc_ref, vl_ref, vc_ref, o_ref, s_scr, m_scr, *, tq, seq):
    n_sub = tq // ATTN_SUB
    keys = min(ATTN_KEYS, seq)
    lat_pieces = [(kl_ref, vl_ref, p * keys, keys) for p in range(seq // keys)]
    pieces = lat_pieces + [(kc_ref, vc_ref, 0, kc_ref.shape[0])]
    offsets = [p * keys for p in range(len(lat_pieces))] + [seq]

    def scores(sb):
        slot = sb % 2
        q = q_ref[sb * ATTN_SUB:(sb + 1) * ATTN_SUB, :]
        state = {"m": None}

        def piece(idx):
            def run():
                k_ref, _, start, size = pieces[idx]
                s = _nt_dot(q, k_ref[start:start + size, :])
                s_scr[slot, :, offsets[idx]:offsets[idx] + size] = s
                row_max = jnp.max(s, axis=-1, keepdims=True)
                state["m"] = row_max if state["m"] is None else jnp.maximum(state["m"], row_max)
                if idx == len(pieces) - 1:
                    m_scr[slot] = state["m"]
            return run
        return [piece(idx) for idx in range(len(pieces))]

    def outputs(sb):
        slot = sb % 2
        state = {"acc": None, "den": None}

        def piece(idx):
            def run():
                _, v_ref, start, size = pieces[idx]
                p = jnp.exp2(s_scr[slot, :, offsets[idx]:offsets[idx] + size] - m_scr[slot])
                den = jnp.sum(p, axis=-1, keepdims=True)
                acc = jnp.dot(p.astype(BF16), v_ref[start:start + size, :], preferred_element_type=F32)
                state["den"] = den if state["den"] is None else state["den"] + den
                state["acc"] = acc if state["acc"] is None else state["acc"] + acc
                if idx == len(pieces) - 1:
                    o_ref[sb * ATTN_SUB:(sb + 1) * ATTN_SUB, :] = (state["acc"] / state["den"]).astype(BF16)
            return run
        return [piece(idx) for idx in range(len(pieces))]

    for run in scores(0):
        run()
    for sb in range(n_sub):
        nxt = scores(sb + 1) if sb + 1 < n_sub else []
        cur = outputs(sb)
        for idx in range(len(pieces)):
            if nxt:
                nxt[idx]()
            cur[idx]()


def _attn_ctx_kernel(q_ref, kc_ref, vc_ref, o_ref):
    s_c = _nt_dot(q_ref[...], kc_ref[...])
    p_c = jnp.exp2(s_c - jnp.max(s_c, axis=-1, keepdims=True))
    o = jnp.dot(p_c.astype(BF16), vc_ref[...], preferred_element_type=F32)
    o_ref[...] = (o / jnp.sum(p_c, axis=-1, keepdims=True)).astype(BF16)


def _attention_latent(q, k, v, batch, seq, ctx_len, t):
    n_lat = batch * seq
    tq = t.tq
    nq = seq // tq
    ctx0 = n_lat // ctx_len
    assert tq % ATTN_SUB == 0 and seq % min(ATTN_KEYS, seq) == 0
    return pl.pallas_call(
        functools.partial(_attn_lat_kernel, tq=tq, seq=seq),
        out_shape=jax.ShapeDtypeStruct((n_lat, MLA_HEADS * V_DIM), BF16),
        grid=(batch, MLA_HEADS, nq),
        in_specs=[
            pl.BlockSpec((None, tq, QK_PAD), lambda b, h, i: (h, b * nq + i, 0)),
            pl.BlockSpec((None, seq, QK_PAD), lambda b, h, i: (h, b, 0)),
            pl.BlockSpec((None, ctx_len, QK_PAD), lambda b, h, i: (h, ctx0 + b, 0)),
            pl.BlockSpec((None, seq, V_DIM), lambda b, h, i: (h, b, 0)),
            pl.BlockSpec((None, ctx_len, V_DIM), lambda b, h, i: (h, ctx0 + b, 0)),
        ],
        out_specs=pl.BlockSpec((tq, V_DIM), lambda b, h, i: (b * nq + i, h)),
        scratch_shapes=[pltpu.VMEM((2, ATTN_SUB, seq + ctx_len), F32), pltpu.VMEM((2, ATTN_SUB, 1), F32)],
        compiler_params=_params(("arbitrary", "arbitrary", "arbitrary")),
        name="attention_latent",
    )(q, k, k, v, v)


def _attention_context(q, k, v, batch, n_lat, ctx_len):
    ctx0 = n_lat // ctx_len
    blk = lambda b, h: (h, ctx0 + b, 0)
    return pl.pallas_call(
        _attn_ctx_kernel,
        out_shape=jax.ShapeDtypeStruct((batch * ctx_len, MLA_HEADS * V_DIM), BF16),
        grid=(batch, MLA_HEADS),
        in_specs=[
            pl.BlockSpec((None, ctx_len, QK_PAD), blk),
            pl.BlockSpec((None, ctx_len, QK_PAD), blk),
            pl.BlockSpec((None, ctx_len, V_DIM), blk),
        ],
        out_specs=pl.BlockSpec((ctx_len, V_DIM), lambda b, h: (b, h)),
        compiler_params=_params(("arbitrary", "arbitrary")),
        name="attention_context",
    )(q, k, v)


def _merge_kernel(attn_ref, bg_ref, z_ref, zp_ref, zn_ref, hp_ref, hn_ref, cw_ref, wao_ref, wco_ref,
                  ga_ref, gc_ref, m_ref, yc_scr, *, tm):
    @pl.when(pl.program_id(1) == 0)
    def _():
        z = z_ref[...].astype(F32)
        rows = lax.broadcasted_iota(jnp.int32, (tm, 1), 0)
        z_prev = jnp.where(rows == 0, zp_ref[BF16_SUBLANES - 1:BF16_SUBLANES, :].astype(F32),
                           pltpu.roll(z, 1, axis=0)) * hp_ref[...]
        z_next = jnp.where(rows == tm - 1, zn_ref[0:1, :].astype(F32),
                           pltpu.roll(z, tm - 1, axis=0)) * hn_ref[...]
        y = cw_ref[0:1, :] * z_prev + cw_ref[1:2, :] * z + cw_ref[2:3, :] * z_next
        yc_scr[...] = (bg_ref[...].astype(F32) * y).astype(BF16)

    y_attn = jnp.dot(attn_ref[...], wao_ref[...], preferred_element_type=F32)
    y_conv = jnp.dot(yc_scr[...], wco_ref[...], preferred_element_type=F32)
    m_ref[...] = (ga_ref[...].astype(F32) * y_attn + gc_ref[...].astype(F32) * y_conv).astype(BF16)


def _merge(attn, bg, z, has_prev, has_next, conv_w, wao, wco, ga, gc, layer, t):
    n, d = attn.shape
    tm, tn = t.tm, t.tn_merge
    halo = tm // BF16_SUBLANES
    last_halo = z.shape[0] // BF16_SUBLANES - 1
    row = lambda i, j: (i, 0)
    return pl.pallas_call(
        functools.partial(_merge_kernel, tm=tm),
        out_shape=jax.ShapeDtypeStruct((n, d), BF16),
        grid=(n // tm, d // tn),
        in_specs=[
            pl.BlockSpec((tm, d), row),
            pl.BlockSpec((tm, d), row),
            pl.BlockSpec((tm, d), row),
            pl.BlockSpec((BF16_SUBLANES, d), lambda i, j: (jnp.maximum(i * halo - 1, 0), 0)),
            pl.BlockSpec((BF16_SUBLANES, d), lambda i, j: (jnp.minimum((i + 1) * halo, last_halo), 0)),
            pl.BlockSpec((tm, 1), row),
            pl.BlockSpec((tm, 1), row),
            pl.BlockSpec((None, 3, d), lambda i, j: (layer, 0, 0)),
            pl.BlockSpec((None, d, tn), lambda i, j: (layer, 0, j)),
            pl.BlockSpec((None, d, tn), lambda i, j: (layer, 0, j)),
            pl.BlockSpec((tm, tn), lambda i, j: (i, j)),
            pl.BlockSpec((tm, tn), lambda i, j: (i, j)),
        ],
        out_specs=pl.BlockSpec((tm, tn), lambda i, j: (i, j)),
        scratch_shapes=[pltpu.VMEM((tm, d), BF16)],
        compiler_params=_params(("arbitrary", "arbitrary")),
        name="merge_branches",
    )(attn, bg, z, z, z, has_prev, has_next, conv_w, wao, wco, ga, gc)


def _outproj_kernel(m_ref, wo_ref, x_ref, g_ref, o_ref):
    y = jnp.dot(m_ref[...], wo_ref[...], preferred_element_type=F32)
    o_ref[...] = x_ref[...] + g_ref[0] * y


def _out_projection(m, wo, xs, mod3, layer, t, brow):
    n, d = m.shape
    tm, tn = t.tm, t.tn_out
    return pl.pallas_call(
        _outproj_kernel,
        out_shape=jax.ShapeDtypeStruct((n, d), F32),
        grid=(n // tm, d // tn),
        in_specs=[
            pl.BlockSpec((tm, d), lambda i, j: (i, 0)),
            pl.BlockSpec((None, d, tn), lambda i, j: (layer, 0, j)),
            pl.BlockSpec((tm, tn), lambda i, j: (i, j)),
            pl.BlockSpec((1, 1, tn), lambda i, j: (brow(i) * 6 + 2, 0, j)),
        ],
        out_specs=pl.BlockSpec((tm, tn), lambda i, j: (i, j)),
        compiler_params=_params(("arbitrary", "arbitrary")),
        name="out_projection",
    )(m, wo, xs, mod3)


def _merge_exchange_pairs(n):
    pairs = []
    t = max(1, (n - 1).bit_length())
    p = 1 << (t - 1)
    while p > 0:
        q, r, d = 1 << (t - 1), 0, p
        while d > 0:
            pairs += [(i, i + d) for i in range(n - d) if (i & p) == r]
            d, q, r = q - p, q >> 1, p
        p >>= 1
    return pairs


def _top_values(s, count):
    groups = [s[g * 8:(g + 1) * 8] for g in range(s.shape[0] // 8)]
    for i, j in _merge_exchange_pairs(len(groups)):
        groups[i], groups[j] = jnp.maximum(groups[i], groups[j]), jnp.minimum(groups[i], groups[j])
    groups = groups[:count] + [jnp.full_like(groups[0], NEG_INF)] * max(0, count - len(groups))
    vals = []
    for t in range(count):
        m = jnp.max(groups[0], axis=0, keepdims=True)
        vals.append(m)
        taken = groups[0] == m
        for k in range(count - 1 - t):
            groups[k] = jnp.where(taken, groups[k + 1], groups[k])
    return vals


def _peer_select_kernel(x_ref, ln_ref, sh_ref, sc_ref, wq_ref, sk_ref, h_ref, e1_ref, thr_ref, e2_ref):
    @pl.when(pl.program_id(1) == 0)
    def _():
        h_ref[...] = _norm_mod(x_ref[...], ln_ref[0], sh_ref[0], sc_ref[0]).astype(BF16)

    qp = jnp.dot(h_ref[...], wq_ref[...], preferred_element_type=F32)
    s1_all = _nt_dot(sk_ref[0], qp[:, :PEER_HALF].astype(BF16))
    s2_all = _nt_dot(sk_ref[1], qp[:, PEER_HALF:].astype(BF16))
    half = PEER_TOPK // 2
    never = 2.0
    for c in range(s1_all.shape[1] // LANES):
        cols = slice(c * LANES, (c + 1) * LANES)
        s1, s2 = s1_all[:, cols], s2_all[:, cols]
        top1 = _top_values(s1, PEER_TOPK)
        top2 = _top_values(s2, PEER_TOPK)
        col1 = jnp.concatenate(top1, axis=0)
        col2 = jnp.concatenate(top2, axis=0)
        cand = jnp.concatenate([top1[0] + col2] + [top1[k] + col2[:half] for k in range(1, half)]
                               + [col1[half:] + top2[0]], axis=0)
        best = _top_values(cand, PEER_TOPK)
        tau = best[PEER_TOPK - 1]
        norm = jnp.zeros_like(tau)
        for b in best:
            norm = norm + jnp.exp(b - best[0])
        e2_top = jnp.exp(col2 - top2[0])
        thr = jnp.full(s1.shape, never, F32)
        for k in range(PEER_TOPK):
            thr_k = jnp.min(jnp.where(top1[k] + col2 >= tau, e2_top, never), axis=0, keepdims=True)
            thr = jnp.where(s1 == top1[k], thr_k, thr)
        e1_ref[:, cols] = jnp.exp(s1 - top1[0]) * (0.5 / norm)
        thr_ref[:, cols] = thr
        e2_ref[:, cols] = jnp.exp(s2 - top2[0])


def _peer_select(xs, ln, mod3, wq, sk, layer, t, brow):
    n, d = xs.shape
    tm = t.tm
    score = jax.ShapeDtypeStruct((PEER_HEADS, N_KEYS, n), F32)
    score_spec = pl.BlockSpec((None, N_KEYS, tm), lambda i, h: (h, 0, i))
    return pl.pallas_call(
        _peer_select_kernel,
        out_shape=(jax.ShapeDtypeStruct((n, d), BF16), score, score, score),
        grid=(n // tm, PEER_HEADS),
        in_specs=[
            pl.BlockSpec((tm, d), lambda i, h: (i, 0)),
            pl.BlockSpec((1, 1, d), lambda i, h: (layer, 0, 0)),
            pl.BlockSpec((1, 1, d), lambda i, h: (brow(i) * 6 + 3, 0, 0)),
            pl.BlockSpec((1, 1, d), lambda i, h: (brow(i) * 6 + 4, 0, 0)),
            pl.BlockSpec((None, d, 2 * PEER_HALF), lambda i, h: (layer, 0, h)),
            pl.BlockSpec((None, None, 2, N_KEYS, PEER_HALF), lambda i, h: (layer, h, 0, 0, 0)),
        ],
        out_specs=(pl.BlockSpec((tm, d), lambda i, h: (i, 0)), score_spec, score_spec, score_spec),
        compiler_params=_params(("arbitrary", "arbitrary")),
        name="peer_select",
    )(xs, ln, mod3, mod3, wq, sk)


def _peer_expert_kernel(h_ref, u_ref, vt_ref, e1_ref, thr_ref, e2_ref, x_ref, g_ref,
                        o_ref, acc_scr, a_scr, p_scr, *, tm, rows_per_block):
    e = pl.program_id(1)

    @pl.when(e == 0)
    def _():
        acc_scr[...] = jnp.zeros_like(acc_scr)

    d = acc_scr.shape[0]
    group = rows_per_block // ROW_GROUPS
    sub = group * N_KEYS
    jh = N_KEYS // KEY_SPLIT
    d_piece = d // VALUE_PIECES

    def hidden(q):
        def run():
            experts = slice(q * sub, (q + 1) * sub)
            a_scr[experts, :] = _nt_dot(u_ref[experts, :], h_ref[...])
        return run

    def values(q, r):
        def run():
            experts = slice(q * sub, (q + 1) * sub)
            rows = slice(r * d_piece, (r + 1) * d_piece)
            acc_scr[rows, :] += jnp.dot(vt_ref[rows, experts], p_scr[experts, :], preferred_element_type=F32)
        return run

    def weights_tile(q, tc, half):
        def run():
            cols = slice(tc * LANES, (tc + 1) * LANES)
            keys = slice(half * jh, (half + 1) * jh)
            weights = [jnp.zeros((jh, LANES), F32) for _ in range(group)]
            for hd in range(PEER_HEADS):
                e2_t = e2_ref[hd, keys, cols]
                for k in range(group):
                    ii = q * group + k
                    picked = e2_t >= thr_ref[hd, ii:ii + 1, cols]
                    weights[k] = weights[k] + jnp.where(picked, e2_t * e1_ref[hd, ii:ii + 1, cols], 0.0)
            for k in range(group):
                r0 = (q * group + k) * N_KEYS + half * jh
                rows = slice(r0, r0 + jh)
                a = a_scr[rows, cols]
                p_scr[rows, cols] = (weights[k] * (a + a * lax.erf(a * (2.0 ** -0.5)))).astype(BF16)
        return run

    hidden(0)()
    for q in range(ROW_GROUPS):
        vpu = [weights_tile(q, tc, half) for tc in range(tm // LANES) for half in range(KEY_SPLIT)]
        mxu = []
        if q + 1 < ROW_GROUPS:
            mxu.append(hidden(q + 1))
        if q > 0:
            mxu += [values(q - 1, r) for r in range(VALUE_PIECES)]
        for k, tile in enumerate(vpu):
            tile()
            for piece in mxu[k * len(mxu) // len(vpu):(k + 1) * len(mxu) // len(vpu)]:
                piece()
    for r in range(VALUE_PIECES):
        values(ROW_GROUPS - 1, r)()

    @pl.when(e == pl.num_programs(1) - 1)
    def _():
        o_ref[...] = x_ref[...] + g_ref[0] * acc_scr[...].T


def _peer_experts(h2, u_tab, vt_tab, e1, thr, e2, xs, mod3, layer, t, brow):
    n, d = xs.shape
    tm, te = t.tm, t.te
    n_blocks = vt_tab.shape[1]
    rows_per_block = te // N_KEYS
    assert rows_per_block % ROW_GROUPS == 0 and rows_per_block % 8 == 0
    token = pl.BlockSpec((tm, d), lambda i, e: (i, 0))
    part = pl.BlockSpec((PEER_HEADS, rows_per_block, tm), lambda i, e: (0, e, i))
    return pl.pallas_call(
        functools.partial(_peer_expert_kernel, tm=tm, rows_per_block=rows_per_block),
        out_shape=jax.ShapeDtypeStruct((n, d), F32),
        grid=(n // tm, n_blocks),
        in_specs=[
            token,
            pl.BlockSpec((None, te, d), lambda i, e: (layer, e, 0)),
            pl.BlockSpec((None, None, d, te), lambda i, e: (layer, e, 0, 0)),
            part, part,
            pl.BlockSpec((PEER_HEADS, N_KEYS, tm), lambda i, e: (0, 0, i)),
            token,
            pl.BlockSpec((1, 1, d), lambda i, e: (brow(i) * 6 + 5, 0, 0)),
        ],
        out_specs=token,
        scratch_shapes=[pltpu.VMEM((d, tm), F32), pltpu.VMEM((te, tm), F32), pltpu.VMEM((te, tm), BF16)],
        compiler_params=_params(("arbitrary", "arbitrary")),
        name="peer_experts",
    )(h2, u_tab, vt_tab, e1, thr, e2, xs, mod3)


def _rope_tables(batch, seq, n_ctx):
    rows = seq // GRID_W
    row = jnp.repeat(jnp.arange(rows), GRID_W).astype(F32)
    col = jnp.tile(jnp.arange(GRID_W), rows).astype(F32)
    half = ROPE_DIM // 2
    inv_freq = ROPE_BASE ** (-jnp.arange(0, half, 2, dtype=F32) / half)
    ang = jnp.concatenate([row[:, None] * inv_freq, col[:, None] * inv_freq], axis=-1)
    cos = jnp.concatenate([jnp.tile(jnp.cos(ang), (batch, 1)), jnp.ones((n_ctx, half), F32)], axis=0)
    sin = jnp.concatenate([jnp.tile(jnp.sin(ang), (batch, 1)), jnp.zeros((n_ctx, half), F32)], axis=0)
    zero = jnp.zeros_like(cos)
    c1 = jnp.concatenate([cos, cos, zero, zero], axis=-1)
    s1 = jnp.concatenate([zero, sin, zero, zero], axis=-1)
    s2 = jnp.concatenate([-sin, zero, zero, zero], axis=-1)
    return c1, s1, s2


def _deinterleave(w):
    return jnp.concatenate([w[..., 0::2], w[..., 1::2]], axis=-1)


def _head_gain(g):
    pad = jnp.zeros(g.shape[:-1] + (QK_PAD - QK_DIM,), F32)
    return jnp.concatenate([g[..., :NOPE_DIM], _deinterleave(g[..., NOPE_DIM:]), pad], axis=-1)[:, None, :]


def kernel(x, c, ctx, c_ctx, w_mod, b_mod, ln_mix, w_in, q_a_norm, kv_a_norm, w_q_up, w_kv_up, q_norm, k_norm,
           conv_w, w_attn_out, w_conv_out, w_o, ln_ffn, w_query, sub_keys, u_experts, v_experts):
    batch, seq, d = x.shape
    ctx_len = ctx.shape[1]
    depth = w_mod.shape[0]
    n_lat, n_ctx = batch * seq, batch * ctx_len
    n_exp = u_experts.shape[1]
    t = _tiles(seq, n_ctx)
    assert n_lat % ctx_len == 0 and seq % GRID_W == 0 and batch < MOD_ROWS and n_exp % t.te == 0

    blocks_per_seq = seq // t.tm
    brow = lambda i: jnp.minimum(i // blocks_per_seq, batch)

    xs = jnp.concatenate([x.reshape(n_lat, d), ctx.reshape(n_ctx, d)], axis=0)
    cc = jnp.concatenate([c, c_ctx[None, :], jnp.zeros((MOD_ROWS - batch - 1, d), F32)], axis=0)
    c1, s1, s2 = _rope_tables(batch, seq, n_ctx)

    pos = jnp.concatenate([jnp.tile(jnp.arange(seq), batch), jnp.tile(jnp.arange(ctx_len), batch)])
    length = jnp.concatenate([jnp.full((n_lat,), seq), jnp.full((n_ctx,), ctx_len)])
    has_prev = (pos != 0).astype(F32)[:, None]
    has_next = (pos != length - 1).astype(F32)[:, None]

    o_kv, o_kr, o_q = 0, KV_LORA, KV_LORA + ROPE_DIM
    o_b = o_q + Q_LORA
    w_small = jnp.concatenate(
        [w_in[..., o_kv:o_kv + KV_LORA], w_in[..., o_q:o_q + Q_LORA], _deinterleave(w_in[..., o_kr:o_kr + ROPE_DIM]),
         jnp.zeros((depth, d, LANES - ROPE_DIM), F32)], axis=-1).astype(BF16)
    w_wide = w_in[..., o_b:].astype(BF16)
    wq = w_q_up.reshape(depth, Q_LORA, MLA_HEADS, QK_DIM)
    wq = jnp.concatenate([wq[..., :NOPE_DIM], _deinterleave(wq[..., NOPE_DIM:]),
                          jnp.zeros((depth, Q_LORA, MLA_HEADS, QK_PAD - QK_DIM), F32)], axis=-1)
    wq = wq.reshape(depth, Q_LORA, MLA_HEADS * QK_PAD).astype(BF16)
    wkv = w_kv_up.astype(BF16)
    wao, wco, wo = w_attn_out.astype(BF16), w_conv_out.astype(BF16), w_o.astype(BF16)
    wqry, sk = w_query.astype(BF16), sub_keys.astype(BF16)
    u_tab = u_experts.astype(BF16)
    vt_tab = v_experts.reshape(depth, n_exp // t.te, t.te, d).transpose(0, 1, 3, 2).astype(BF16)
    qg, kg = _head_gain(q_norm), _head_gain(k_norm)
    ln_mix3, ln_ffn3 = ln_mix[:, None, :], ln_ffn[:, None, :]
    qan3, kvan3 = q_a_norm[:, None, :], kv_a_norm[:, None, :]
    b_mod3 = b_mod[:, None, :]

    for l in range(depth):
        last = l == depth - 1
        mod3 = _ada_params(cc, w_mod, b_mod3, l).reshape(MOD_ROWS * 6, 1, d)

        ps, bg, z, ga, gc = _in_projection(xs, ln_mix3, mod3, w_small, w_wide, l, t, brow)
        q, k, v = _qkv_heads(ps, qan3, kvan3, wq, wkv, qg, kg, c1, s1, s2, l, t)
        attn = _attention_latent(q, k, v, batch, seq, ctx_len, t)
        if not last:
            attn = jnp.concatenate([attn, _attention_context(q, k, v, batch, n_lat, ctx_len)], axis=0)
        m = _merge(attn, bg, z, has_prev, has_next, conv_w, wao, wco, ga, gc, l, t)
        xs = _out_projection(m, wo, xs, mod3, l, t, brow)

        h2, pe1, pthr, pe2 = _peer_select(xs, ln_ffn3, mod3, wqry, sk, l, t, brow)
        xs = _peer_experts(h2, u_tab, vt_tab, pe1, pthr, pe2, xs, mod3, l, t, brow)

    return xs[:n_lat].reshape(batch, seq, d)
```

```python
import functools
from typing import NamedTuple

import jax
import jax.numpy as jnp
from jax import lax
from jax.experimental import pallas as pl
from jax.experimental.pallas import tpu as pltpu

F32 = jnp.float32
BF16 = jnp.bfloat16

GRID_W = 64
MLA_HEADS = 16
Q_LORA = 512
KV_LORA = 256
NOPE_DIM = 128
ROPE_DIM = 64
V_DIM = 128
QK_DIM = NOPE_DIM + ROPE_DIM
QK_PAD = 256
ATTN_SCALE = QK_DIM ** -0.5
LOG2_E = 1.4426950408889634
ROPE_BASE = 10000.0
PEER_HEADS = 8
PEER_HALF = 128
N_KEYS = 128
PEER_TOPK = 16
EPS = 1e-6
MOD_ROWS = 8
LANES = 128
BF16_SUBLANES = 16
VMEM_LIMIT = 56 * 1024 * 1024
NEG_INF = float("-inf")
ROW_GROUPS = 4
VALUE_PIECES = 4
KEY_SPLIT = 4
ATTN_SUB = 256
ATTN_KEYS = 1024
SMALL_COLS = KV_LORA + Q_LORA + LANES


class Tiles(NamedTuple):
    tm: int
    tn_in: int
    tn_merge: int
    tn_out: int
    te: int
    tq: int


def _tiles(seq, n_ctx):
    tm = 512
    assert seq % tm == 0 and n_ctx % tm == 0
    return Tiles(tm=tm, tn_in=512, tn_merge=512, tn_out=1024, te=ROW_GROUPS * 2 * N_KEYS, tq=min(8 * ATTN_SUB, seq))


def _params(sem, flags=None):
    return pltpu.CompilerParams(dimension_semantics=sem, vmem_limit_bytes=VMEM_LIMIT, flags=flags)


def _nt_dot(a, b):
    return lax.dot_general(a, b, (((1,), (1,)), ((), ())), preferred_element_type=F32)


def _ada_kernel(c_ref, w_ref, b_ref, o_ref):
    a = c_ref[...]
    a = a / (1.0 + jnp.exp(-a))
    o_ref[...] = jnp.dot(a.astype(BF16), w_ref[...].astype(BF16), preferred_element_type=F32) + b_ref[...]


def _ada_params(cc, w_mod, b_mod, layer):
    rows, d = cc.shape
    cols = w_mod.shape[2]
    tn = 1024
    return pl.pallas_call(
        _ada_kernel,
        out_shape=jax.ShapeDtypeStruct((rows, cols), F32),
        grid=(cols // tn,),
        in_specs=[
            pl.BlockSpec((rows, d), lambda j: (0, 0)),
            pl.BlockSpec((None, d, tn), lambda j: (layer, 0, j)),
            pl.BlockSpec((None, 1, tn), lambda j: (layer, 0, j)),
        ],
        out_specs=pl.BlockSpec((rows, tn), lambda j: (0, j)),
        compiler_params=_params(("arbitrary",)),
        name="ada_params",
    )(cc, w_mod, b_mod)


def _norm_mod(x, gain, shift, scale):
    ms = jnp.mean(x * x, axis=-1, keepdims=True)
    y = x * lax.rsqrt(ms + EPS) * gain
    return y * (1.0 + scale) + shift


def _inproj_kernel(x_ref, ln_ref, sh_ref, sc_ref, ws_ref, wb_ref, wc_ref, wu_ref, wga_ref, wgc_ref,
                   ps_ref, b_ref, z_ref, ga_ref, gc_ref, h_scr):
    @pl.when(pl.program_id(1) == 0)
    def _():
        h = _norm_mod(x_ref[...], ln_ref[0], sh_ref[0], sc_ref[0]).astype(BF16)
        h_scr[...] = h
        ps_ref[...] = jnp.dot(h, ws_ref[...], preferred_element_type=F32)

    h = h_scr[...]
    proj = lambda wt_ref: _nt_dot(h, wt_ref[...])
    b_ref[...] = proj(wb_ref).astype(BF16)
    z_ref[...] = (proj(wc_ref) * proj(wu_ref)).astype(BF16)
    ga_ref[...] = (1.0 / (1.0 + jnp.exp(-proj(wga_ref)))).astype(BF16)
    gc_ref[...] = (1.0 / (1.0 + jnp.exp(-proj(wgc_ref)))).astype(BF16)


def _in_projection(xs, ln, mod3, w_small, w_wide, layer, t, brow):
    n, d = xs.shape
    tm, tn = t.tm, t.tn_in
    wide = pl.BlockSpec((tm, tn), lambda i, j: (i, j))
    out_wide = jax.ShapeDtypeStruct((n, d), BF16)
    segment = lambda s: pl.BlockSpec((None, tn, d), lambda i, j: (layer, s * (d // tn) + j, 0))
    return pl.pallas_call(
        _inproj_kernel,
        out_shape=(jax.ShapeDtypeStruct((n, SMALL_COLS), F32), out_wide, out_wide, out_wide, out_wide),
        grid=(n // tm, d // tn),
        in_specs=[
            pl.BlockSpec((tm, d), lambda i, j: (i, 0)),
            pl.BlockSpec((1, 1, d), lambda i, j: (layer, 0, 0)),
            pl.BlockSpec((1, 1, d), lambda i, j: (brow(i) * 6 + 0, 0, 0)),
            pl.BlockSpec((1, 1, d), lambda i, j: (brow(i) * 6 + 1, 0, 0)),
            pl.BlockSpec((None, d, SMALL_COLS), lambda i, j: (layer, 0, 0)),
            segment(0), segment(1), segment(2), segment(3), segment(4),
        ],
        out_specs=(pl.BlockSpec((tm, SMALL_COLS), lambda i, j: (i, 0)), wide, wide, wide, wide),
        scratch_shapes=[pltpu.VMEM((tm, d), BF16)],
        compiler_params=_params(("arbitrary", "arbitrary")),
        name="in_projection",
    )(xs, ln, mod3, mod3, w_small, w_wide, w_wide, w_wide, w_wide, w_wide)


def _rope(r, c1, s1, s2):
    return r * c1 + pltpu.roll(r, 32, axis=1) * s1 + pltpu.roll(r, 96, axis=1) * s2


def _rms(x, true_width):
    return lax.rsqrt(jnp.sum(x * x, axis=-1, keepdims=True) * (1.0 / true_width) + EPS)


def _qkv_kernel(ps_ref, qan_ref, kvan_ref, wq_ref, wkv_ref, qg_ref, kg_ref, c1_ref, s1_ref, s2_ref,
                q_ref, k_ref, v_ref):
    c1, s1, s2 = c1_ref[...], s1_ref[...], s2_ref[...]
    kv_lat = ps_ref[:, 0:KV_LORA]
    q_lat = ps_ref[:, KV_LORA:KV_LORA + Q_LORA]
    kr = ps_ref[:, KV_LORA + Q_LORA:SMALL_COLS]
    kvn = (kv_lat * _rms(kv_lat, KV_LORA) * kvan_ref[0]).astype(BF16)
    qn = (q_lat * _rms(q_lat, Q_LORA) * qan_ref[0]).astype(BF16)
    kr_ssq = jnp.sum(kr * kr, axis=-1, keepdims=True)
    qg, kg = qg_ref[0], kg_ref[0]
    kr_rot = _rope(kr * kg[:, NOPE_DIM:], c1, s1, s2)
    for h in range(MLA_HEADS):
        qh = jnp.dot(qn, wq_ref[:, h * QK_PAD:(h + 1) * QK_PAD], preferred_element_type=F32)
        qh = qh * _rms(qh, QK_DIM) * qg
        q_rot = _rope(qh[:, NOPE_DIM:], c1, s1, s2)
        q_ref[h] = (jnp.concatenate([qh[:, :NOPE_DIM], q_rot], axis=-1) * (ATTN_SCALE * LOG2_E)).astype(BF16)

        kvh = jnp.dot(kvn, wkv_ref[:, h * QK_PAD:(h + 1) * QK_PAD], preferred_element_type=F32)
        k_nope = kvh[:, :NOPE_DIM]
        ssq = jnp.sum(k_nope * k_nope, axis=-1, keepdims=True) + kr_ssq
        r = lax.rsqrt(ssq * (1.0 / QK_DIM) + EPS)
        k_ref[h] = (jnp.concatenate([k_nope * kg[:, :NOPE_DIM], kr_rot], axis=-1) * r).astype(BF16)
        v_ref[h] = kvh[:, NOPE_DIM:].astype(BF16)


def _qkv_heads(ps, qan, kvan, wq, wkv, qg, kg, c1, s1, s2, layer, t):
    n = ps.shape[0]
    tm = t.tm
    row = lambda i: (i, 0)
    per_layer = lambda i: (layer, 0, 0)
    out = lambda i: (0, i, 0)
    return pl.pallas_call(
        _qkv_kernel,
        out_shape=(jax.ShapeDtypeStruct((MLA_HEADS, n, QK_PAD), BF16),
                   jax.ShapeDtypeStruct((MLA_HEADS, n, QK_PAD), BF16),
                   jax.ShapeDtypeStruct((MLA_HEADS, n, V_DIM), BF16)),
        grid=(n // tm,),
        in_specs=[
            pl.BlockSpec((tm, SMALL_COLS), row),
            pl.BlockSpec((1, 1, Q_LORA), per_layer),
            pl.BlockSpec((1, 1, KV_LORA), per_layer),
            pl.BlockSpec((None, Q_LORA, MLA_HEADS * QK_PAD), per_layer),
            pl.BlockSpec((None, KV_LORA, MLA_HEADS * QK_PAD), per_layer),
            pl.BlockSpec((1, 1, QK_PAD), per_layer),
            pl.BlockSpec((1, 1, QK_PAD), per_layer),
            pl.BlockSpec((tm, LANES), row),
            pl.BlockSpec((tm, LANES), row),
            pl.BlockSpec((tm, LANES), row),
        ],
        out_specs=(pl.BlockSpec((MLA_HEADS, tm, QK_PAD), out),
                   pl.BlockSpec((MLA_HEADS, tm, QK_PAD), out),
                   pl.BlockSpec((MLA_HEADS, tm, V_DIM), out)),
        compiler_params=_params(("arbitrary",)),
        name="qkv_heads",
    )(ps, qan, kvan, wq, wkv, qg, kg, c1, s1, s2)


def _attn_lat_kernel(q_ref, kl_ref, kc_ref, vl_ref, vc_ref, o_ref, s_scr, m_scr, *, tq, seq):
    n_sub = tq // ATTN_SUB
    keys = min(ATTN_KEYS, seq)
    lat_pieces = [(kl_ref, vl_ref, p * keys, keys) for p in range(seq // keys)]
    pieces = lat_pieces + [(kc_ref, vc_ref, 0, kc_ref.shape[0])]
    offsets = [p * keys for p in range(len(lat_pieces))] + [seq]

    def scores(sb):
        slot = sb % 2
        q = q_ref[sb * ATTN_SUB:(sb + 1) * ATTN_SUB, :]
        state = {"m": None}

        def piece(idx):
            def run():
                k_ref, _, start, size = pieces[idx]
                s = _nt_dot(q, k_ref[start:start + size, :])
                s_scr[slot, :, offsets[idx]:offsets[idx] + size] = s
                row_max = jnp.max(s, axis=-1, keepdims=True)
                state["m"] = row_max if state["m"] is None else jnp.maximum(state["m"], row_max)
                if idx == len(pieces) - 1:
                    m_scr[slot] = state["m"]
            return run
        return [piece(idx) for idx in range(len(pieces))]

    def outputs(sb):
        slot = sb % 2
        state = {"acc": None, "den": None}

        def piece(idx):
            def run():
                _, v_ref, start, size = pieces[idx]
                p = jnp.exp2(s_scr[slot, :, offsets[idx]:offsets[idx] + size] - m_scr[slot])
                den = jnp.sum(p, axis=-1, keepdims=True)
                acc = jnp.dot(p.astype(BF16), v_ref[start:start + size, :], preferred_element_type=F32)
                state["den"] = den if state["den"] is None else state["den"] + den
                state["acc"] = acc if state["acc"] is None else state["acc"] + acc
                if idx == len(pieces) - 1:
                    o_ref[sb * ATTN_SUB:(sb + 1) * ATTN_SUB, :] = (state["acc"] / state["den"]).astype(BF16)
            return run
        return [piece(idx) for idx in range(len(pieces))]

    for run in scores(0):
        run()
    for sb in range(n_sub):
        nxt = scores(sb + 1) if sb + 1 < n_sub else []
        cur = outputs(sb)
        for idx in range(len(pieces)):
            if nxt:
                nxt[idx]()
            cur[idx]()


def _attn_ctx_kernel(q_ref, kc_ref, vc_ref, o_ref):
    s_c = _nt_dot(q_ref[...], kc_ref[...])
    p_c = jnp.exp2(s_c - jnp.max(s_c, axis=-1, keepdims=True))
    o = jnp.dot(p_c.astype(BF16), vc_ref[...], preferred_element_type=F32)
    o_ref[...] = (o / jnp.sum(p_c, axis=-1, keepdims=True)).astype(BF16)


def _attention_latent(q, k, v, batch, seq, ctx_len, t):
    n_lat = batch * seq
    tq = t.tq
    nq = seq // tq
    ctx0 = n_lat // ctx_len
    assert tq % ATTN_SUB == 0 and seq % min(ATTN_KEYS, seq) == 0
    return pl.pallas_call(
        functools.partial(_attn_lat_kernel, tq=tq, seq=seq),
        out_shape=jax.ShapeDtypeStruct((n_lat, MLA_HEADS * V_DIM), BF16),
        grid=(batch, MLA_HEADS, nq),
        in_specs=[
            pl.BlockSpec((None, tq, QK_PAD), lambda b, h, i: (h, b * nq + i, 0)),
            pl.BlockSpec((None, seq, QK_PAD), lambda b, h, i: (h, b, 0)),
            pl.BlockSpec((None, ctx_len, QK_PAD), lambda b, h, i: (h, ctx0 + b, 0)),
            pl.BlockSpec((None, seq, V_DIM), lambda b, h, i: (h, b, 0)),
            pl.BlockSpec((None, ctx_len, V_DIM), lambda b, h, i: (h, ctx0 + b, 0)),
        ],
        out_specs=pl.BlockSpec((tq, V_DIM), lambda b, h, i: (b * nq + i, h)),
        scratch_shapes=[pltpu.VMEM((2, ATTN_SUB, seq + ctx_len), F32), pltpu.VMEM((2, ATTN_SUB, 1), F32)],
        compiler_params=_params(("arbitrary", "arbitrary", "arbitrary")),
        name="attention_latent",
    )(q, k, k, v, v)


def _attention_context(q, k, v, batch, n_lat, ctx_len):
    ctx0 = n_lat // ctx_len
    blk = lambda b, h: (h, ctx0 + b, 0)
    return pl.pallas_call(
        _attn_ctx_kernel,
        out_shape=jax.ShapeDtypeStruct((batch * ctx_len, MLA_HEADS * V_DIM), BF16),
        grid=(batch, MLA_HEADS),
        in_specs=[
            pl.BlockSpec((None, ctx_len, QK_PAD), blk),
            pl.BlockSpec((None, ctx_len, QK_PAD), blk),
            pl.BlockSpec((None, ctx_len, V_DIM), blk),
        ],
        out_specs=pl.BlockSpec((ctx_len, V_DIM), lambda b, h: (b, h)),
        compiler_params=_params(("arbitrary", "arbitrary")),
        name="attention_context",
    )(q, k, v)


def _merge_kernel(attn_ref, bg_ref, z_ref, zp_ref, zn_ref, hp_ref, hn_ref, cw_ref, wao_ref, wco_ref,
                  ga_ref, gc_ref, m_ref, yc_scr, *, tm):
    @pl.when(pl.program_id(1) == 0)
    def _():
        z = z_ref[...].astype(F32)
        rows = lax.broadcasted_iota(jnp.int32, (tm, 1), 0)
        z_prev = jnp.where(rows == 0, zp_ref[BF16_SUBLANES - 1:BF16_SUBLANES, :].astype(F32),
                           pltpu.roll(z, 1, axis=0)) * hp_ref[...]
        z_next = jnp.where(rows == tm - 1, zn_ref[0:1, :].astype(F32),
                           pltpu.roll(z, tm - 1, axis=0)) * hn_ref[...]
        y = cw_ref[0:1, :] * z_prev + cw_ref[1:2, :] * z + cw_ref[2:3, :] * z_next
        yc_scr[...] = (bg_ref[...].astype(F32) * y).astype(BF16)

    y_attn = jnp.dot(attn_ref[...], wao_ref[...], preferred_element_type=F32)
    y_conv = jnp.dot(yc_scr[...], wco_ref[...], preferred_element_type=F32)
    m_ref[...] = (ga_ref[...].astype(F32) * y_attn + gc_ref[...].astype(F32) * y_conv).astype(BF16)


def _merge(attn, bg, z, has_prev, has_next, conv_w, wao, wco, ga, gc, layer, t):
    n, d = attn.shape
    tm, tn = t.tm, t.tn_merge
    halo = tm // BF16_SUBLANES
    last_halo = z.shape[0] // BF16_SUBLANES - 1
    row = lambda i, j: (i, 0)
    return pl.pallas_call(
        functools.partial(_merge_kernel, tm=tm),
        out_shape=jax.ShapeDtypeStruct((n, d), BF16),
        grid=(n // tm, d // tn),
        in_specs=[
            pl.BlockSpec((tm, d), row),
            pl.BlockSpec((tm, d), row),
            pl.BlockSpec((tm, d), row),
            pl.BlockSpec((BF16_SUBLANES, d), lambda i, j: (jnp.maximum(i * halo - 1, 0), 0)),
            pl.BlockSpec((BF16_SUBLANES, d), lambda i, j: (jnp.minimum((i + 1) * halo, last_halo), 0)),
            pl.BlockSpec((tm, 1), row),
            pl.BlockSpec((tm, 1), row),
            pl.BlockSpec((None, 3, d), lambda i, j: (layer, 0, 0)),
            pl.BlockSpec((None, d, tn), lambda i, j: (layer, 0, j)),
            pl.BlockSpec((None, d, tn), lambda i, j: (layer, 0, j)),
            pl.BlockSpec((tm, tn), lambda i, j: (i, j)),
            pl.BlockSpec((tm, tn), lambda i, j: (i, j)),
        ],
        out_specs=pl.BlockSpec((tm, tn), lambda i, j: (i, j)),
        scratch_shapes=[pltpu.VMEM((tm, d), BF16)],
        compiler_params=_params(("arbitrary", "arbitrary")),
        name="merge_branches",
    )(attn, bg, z, z, z, has_prev, has_next, conv_w, wao, wco, ga, gc)


def _outproj_kernel(m_ref, wo_ref, x_ref, g_ref, o_ref):
    y = jnp.dot(m_ref[...], wo_ref[...], preferred_element_type=F32)
    o_ref[...] = x_ref[...] + g_ref[0] * y


def _out_projection(m, wo, xs, mod3, layer, t, brow):
    n, d = m.shape
    tm, tn = t.tm, t.tn_out
    return pl.pallas_call(
        _outproj_kernel,
        out_shape=jax.ShapeDtypeStruct((n, d), F32),
        grid=(n // tm, d // tn),
        in_specs=[
            pl.BlockSpec((tm, d), lambda i, j: (i, 0)),
            pl.BlockSpec((None, d, tn), lambda i, j: (layer, 0, j)),
            pl.BlockSpec((tm, tn), lambda i, j: (i, j)),
            pl.BlockSpec((1, 1, tn), lambda i, j: (brow(i) * 6 + 2, 0, j)),
        ],
        out_specs=pl.BlockSpec((tm, tn), lambda i, j: (i, j)),
        compiler_params=_params(("arbitrary", "arbitrary")),
        name="out_projection",
    )(m, wo, xs, mod3)


def _merge_exchange_pairs(n):
    pairs = []
    t = max(1, (n - 1).bit_length())
    p = 1 << (t - 1)
    while p > 0:
        q, r, d = 1 << (t - 1), 0, p
        while d > 0:
            pairs += [(i, i + d) for i in range(n - d) if (i & p) == r]
            d, q, r = q - p, q >> 1, p
        p >>= 1
    return pairs


def _top_values(s, count):
    groups = [s[g * 8:(g + 1) * 8] for g in range(s.shape[0] // 8)]
    for i, j in _merge_exchange_pairs(len(groups)):
        groups[i], groups[j] = jnp.maximum(groups[i], groups[j]), jnp.minimum(groups[i], groups[j])
    groups = groups[:count] + [jnp.full_like(groups[0], NEG_INF)] * max(0, count - len(groups))
    vals = []
    for t in range(count):
        m = jnp.max(groups[0], axis=0, keepdims=True)
        vals.append(m)
        taken = groups[0] == m
        for k in range(count - 1 - t):
            groups[k] = jnp.where(taken, groups[k + 1], groups[k])
    return vals


def _peer_select_kernel(x_ref, ln_ref, sh_ref, sc_ref, wq_ref, sk_ref, h_ref, e1_ref, thr_ref, e2_ref):
    @pl.when(pl.program_id(1) == 0)
    def _():
        h_ref[...] = _norm_mod(x_ref[...], ln_ref[0], sh_ref[0], sc_ref[0]).astype(BF16)

    qp = jnp.dot(h_ref[...], wq_ref[...], preferred_element_type=F32)
    s1_all = _nt_dot(sk_ref[0], qp[:, :PEER_HALF].astype(BF16))
    s2_all = _nt_dot(sk_ref[1], qp[:, PEER_HALF:].astype(BF16))
    half = PEER_TOPK // 2
    never = 2.0
    for c in range(s1_all.shape[1] // LANES):
        cols = slice(c * LANES, (c + 1) * LANES)
        s1, s2 = s1_all[:, cols], s2_all[:, cols]
        top1 = _top_values(s1, PEER_TOPK)
        top2 = _top_values(s2, PEER_TOPK)
        col1 = jnp.concatenate(top1, axis=0)
        col2 = jnp.concatenate(top2, axis=0)
        cand = jnp.concatenate([top1[0] + col2] + [top1[k] + col2[:half] for k in range(1, half)]
                               + [col1[half:] + top2[0]], axis=0)
        best = _top_values(cand, PEER_TOPK)
        tau = best[PEER_TOPK - 1]
        norm = jnp.zeros_like(tau)
        for b in best:
            norm = norm + jnp.exp(b - best[0])
        e2_top = jnp.exp(col2 - top2[0])
        thr = jnp.full(s1.shape, never, F32)
        for k in range(PEER_TOPK):
            thr_k = jnp.min(jnp.where(top1[k] + col2 >= tau, e2_top, never), axis=0, keepdims=True)
            thr = jnp.where(s1 == top1[k], thr_k, thr)
        e1_ref[:, cols] = jnp.exp(s1 - top1[0]) * (0.5 / norm)
        thr_ref[:, cols] = thr
        e2_ref[:, cols] = jnp.exp(s2 - top2[0])


def _peer_select(xs, ln, mod3, wq, sk, layer, t, brow):
    n, d = xs.shape
    tm = t.tm
    score = jax.ShapeDtypeStruct((PEER_HEADS, N_KEYS, n), F32)
    score_spec = pl.BlockSpec((None, N_KEYS, tm), lambda i, h: (h, 0, i))
    return pl.pallas_call(
        _peer_select_kernel,
        out_shape=(jax.ShapeDtypeStruct((n, d), BF16), score, score, score),
        grid=(n // tm, PEER_HEADS),
        in_specs=[
            pl.BlockSpec((tm, d), lambda i, h: (i, 0)),
            pl.BlockSpec((1, 1, d), lambda i, h: (layer, 0, 0)),
            pl.BlockSpec((1, 1, d), lambda i, h: (brow(i) * 6 + 3, 0, 0)),
            pl.BlockSpec((1, 1, d), lambda i, h: (brow(i) * 6 + 4, 0, 0)),
            pl.BlockSpec((None, d, 2 * PEER_HALF), lambda i, h: (layer, 0, h)),
            pl.BlockSpec((None, None, 2, N_KEYS, PEER_HALF), lambda i, h: (layer, h, 0, 0, 0)),
        ],
        out_specs=(pl.BlockSpec((tm, d), lambda i, h: (i, 0)), score_spec, score_spec, score_spec),
        compiler_params=_params(("arbitrary", "arbitrary")),
        name="peer_select",
    )(xs, ln, mod3, mod3, wq, sk)


def _peer_expert_kernel(h_ref, u_ref, vt_ref, e1_ref, thr_ref, e2_ref, x_ref, g_ref,
                        o_ref, acc_scr, a_scr, p_scr, *, tm, rows_per_block):
    e = pl.program_id(1)

    @pl.when(e == 0)
    def _():
        acc_scr[...] = jnp.zeros_like(acc_scr)

    d = acc_scr.shape[0]
    group = rows_per_block // ROW_GROUPS
    sub = group * N_KEYS
    jh = N_KEYS // KEY_SPLIT
    d_piece = d // VALUE_PIECES

    def hidden(q):
        def run():
            experts = slice(q * sub, (q + 1) * sub)
            a_scr[experts, :] = _nt_dot(u_ref[experts, :], h_ref[...])
        return run

    def values(q, r):
        def run():
            experts = slice(q * sub, (q + 1) * sub)
            rows = slice(r * d_piece, (r + 1) * d_piece)
            acc_scr[rows, :] += jnp.dot(vt_ref[rows, experts], p_scr[experts, :], preferred_element_type=F32)
        return run

    def weights_tile(q, tc, half):
        def run():
            cols = slice(tc * LANES, (tc + 1) * LANES)
            keys = slice(half * jh, (half + 1) * jh)
            weights = [jnp.zeros((jh, LANES), F32) for _ in range(group)]
            for hd in range(PEER_HEADS):
                e2_t = e2_ref[hd, keys, cols]
                for k in range(group):
                    ii = q * group + k
                    picked = e2_t >= thr_ref[hd, ii:ii + 1, cols]
                    weights[k] = weights[k] + jnp.where(picked, e2_t * e1_ref[hd, ii:ii + 1, cols], 0.0)
            for k in range(group):
                r0 = (q * group + k) * N_KEYS + half * jh
                rows = slice(r0, r0 + jh)
                a = a_scr[rows, cols]
                p_scr[rows, cols] = (weights[k] * (a + a * lax.erf(a * (2.0 ** -0.5)))).astype(BF16)
        return run

    hidden(0)()
    for q in range(ROW_GROUPS):
        vpu = [weights_tile(q, tc, half) for tc in range(tm // LANES) for half in range(KEY_SPLIT)]
        mxu = []
        if q + 1 < ROW_GROUPS:
            mxu.append(hidden(q + 1))
        if q > 0:
            mxu += [values(q - 1, r) for r in range(VALUE_PIECES)]
        for k, tile in enumerate(vpu):
            tile()
            for piece in mxu[k * len(mxu) // len(vpu):(k + 1) * len(mxu) // len(vpu)]:
                piece()
    for r in range(VALUE_PIECES):
        values(ROW_GROUPS - 1, r)()

    @pl.when(e == pl.num_programs(1) - 1)
    def _():
        o_ref[...] = x_ref[...] + g_ref[0] * acc_scr[...].T


def _peer_experts(h2, u_tab, vt_tab, e1, thr, e2, xs, mod3, layer, t, brow):
    n, d = xs.shape
    tm, te = t.tm, t.te
    n_blocks = vt_tab.shape[1]
    rows_per_block = te // N_KEYS
    assert rows_per_block % ROW_GROUPS == 0 and rows_per_block % 8 == 0
    token = pl.BlockSpec((tm, d), lambda i, e: (i, 0))
    part = pl.BlockSpec((PEER_HEADS, rows_per_block, tm), lambda i, e: (0, e, i))
    return pl.pallas_call(
        functools.partial(_peer_expert_kernel, tm=tm, rows_per_block=rows_per_block),
        out_shape=jax.ShapeDtypeStruct((n, d), F32),
        grid=(n // tm, n_blocks),
        in_specs=[
            token,
            pl.BlockSpec((None, te, d), lambda i, e: (layer, e, 0)),
            pl.BlockSpec((None, None, d, te), lambda i, e: (layer, e, 0, 0)),
            part, part,
            pl.BlockSpec((PEER_HEADS, N_KEYS, tm), lambda i, e: (0, 0, i)),
            token,
            pl.BlockSpec((1, 1, d), lambda i, e: (brow(i) * 6 + 5, 0, 0)),
        ],
        out_specs=token,
        scratch_shapes=[pltpu.VMEM((d, tm), F32), pltpu.VMEM((te, tm), F32), pltpu.VMEM((te, tm), BF16)],
        compiler_params=_params(("arbitrary", "arbitrary")),
        name="peer_experts",
    )(h2, u_tab, vt_tab, e1, thr, e2, xs, mod3)


def _rope_tables(batch, seq, n_ctx):
    rows = seq // GRID_W
    row = jnp.repeat(jnp.arange(rows), GRID_W).astype(F32)
    col = jnp.tile(jnp.arange(GRID_W), rows).astype(F32)
    half = ROPE_DIM // 2
    inv_freq = ROPE_BASE ** (-jnp.arange(0, half, 2, dtype=F32) / half)
    ang = jnp.concatenate([row[:, None] * inv_freq, col[:, None] * inv_freq], axis=-1)
    cos = jnp.concatenate([jnp.tile(jnp.cos(ang), (batch, 1)), jnp.ones((n_ctx, half), F32)], axis=0)
    sin = jnp.concatenate([jnp.tile(jnp.sin(ang), (batch, 1)), jnp.zeros((n_ctx, half), F32)], axis=0)
    zero = jnp.zeros_like(cos)
    c1 = jnp.concatenate([cos, cos, zero, zero], axis=-1)
    s1 = jnp.concatenate([zero, sin, zero, zero], axis=-1)
    s2 = jnp.concatenate([-sin, zero, zero, zero], axis=-1)
    return c1, s1, s2


def _deinterleave(w):
    return jnp.concatenate([w[..., 0::2], w[..., 1::2]], axis=-1)


def _head_gain(g):
    pad = jnp.zeros(g.shape[:-1] + (QK_PAD - QK_DIM,), F32)
    return jnp.concatenate([g[..., :NOPE_DIM], _deinterleave(g[..., NOPE_DIM:]), pad], axis=-1)[:, None, :]


def kernel(x, c, ctx, c_ctx, w_mod, b_mod, ln_mix, w_in, q_a_norm, kv_a_norm, w_q_up, w_kv_up, q_norm, k_norm,
           conv_w, w_attn_out, w_conv_out, w_o, ln_ffn, w_query, sub_keys, u_experts, v_experts):
    batch, seq, d = x.shape
    ctx_len = ctx.shape[1]
    depth = w_mod.shape[0]
    n_lat, n_ctx = batch * seq, batch * ctx_len
    n_exp = u_experts.shape[1]
    t = _tiles(seq, n_ctx)
    assert n_lat % ctx_len == 0 and seq % GRID_W == 0 and batch < MOD_ROWS and n_exp % t.te == 0

    blocks_per_seq = seq // t.tm
    brow = lambda i: jnp.minimum(i // blocks_per_seq, batch)

    xs = jnp.concatenate([x.reshape(n_lat, d), ctx.reshape(n_ctx, d)], axis=0)
    cc = jnp.concatenate([c, c_ctx[None, :], jnp.zeros((MOD_ROWS - batch - 1, d), F32)], axis=0)
    c1, s1, s2 = _rope_tables(batch, seq, n_ctx)

    pos = jnp.concatenate([jnp.tile(jnp.arange(seq), batch), jnp.tile(jnp.arange(ctx_len), batch)])
    length = jnp.concatenate([jnp.full((n_lat,), seq), jnp.full((n_ctx,), ctx_len)])
    has_prev = (pos != 0).astype(F32)[:, None]
    has_next = (pos != length - 1).astype(F32)[:, None]

    o_kv, o_kr, o_q = 0, KV_LORA, KV_LORA + ROPE_DIM
    o_b = o_q + Q_LORA
    w_small = jnp.concatenate(
        [w_in[..., o_kv:o_kv + KV_LORA], w_in[..., o_q:o_q + Q_LORA], _deinterleave(w_in[..., o_kr:o_kr + ROPE_DIM]),
         jnp.zeros((depth, d, LANES - ROPE_DIM), F32)], axis=-1).astype(BF16)
    w_wide = jnp.swapaxes(w_in[..., o_b:], 1, 2).astype(BF16)
    wq = w_q_up.reshape(depth, Q_LORA, MLA_HEADS, QK_DIM)
    wq = jnp.concatenate([wq[..., :NOPE_DIM], _deinterleave(wq[..., NOPE_DIM:]),
                          jnp.zeros((depth, Q_LORA, MLA_HEADS, QK_PAD - QK_DIM), F32)], axis=-1)
    wq = wq.reshape(depth, Q_LORA, MLA_HEADS * QK_PAD).astype(BF16)
    wkv = w_kv_up.astype(BF16)
    wao, wco, wo = w_attn_out.astype(BF16), w_conv_out.astype(BF16), w_o.astype(BF16)
    wqry, sk = w_query.astype(BF16), sub_keys.astype(BF16)
    u_tab = u_experts.astype(BF16)
    vt_tab = v_experts.reshape(depth, n_exp // t.te, t.te, d).transpose(0, 1, 3, 2).astype(BF16)
    qg, kg = _head_gain(q_norm), _head_gain(k_norm)
    ln_mix3, ln_ffn3 = ln_mix[:, None, :], ln_ffn[:, None, :]
    qan3, kvan3 = q_a_norm[:, None, :], kv_a_norm[:, None, :]
    b_mod3 = b_mod[:, None, :]

    for l in range(depth):
        last = l == depth - 1
        mod3 = _ada_params(cc, w_mod, b_mod3, l).reshape(MOD_ROWS * 6, 1, d)

        ps, bg, z, ga, gc = _in_projection(xs, ln_mix3, mod3, w_small, w_wide, l, t, brow)
        q, k, v = _qkv_heads(ps, qan3, kvan3, wq, wkv, qg, kg, c1, s1, s2, l, t)
        attn = _attention_latent(q, k, v, batch, seq, ctx_len, t)
        if not last:
            attn = jnp.concatenate([attn, _attention_context(q, k, v, batch, n_lat, ctx_len)], axis=0)
        m = _merge(attn, bg, z, has_prev, has_next, conv_w, wao, wco, ga, gc, l, t)
        xs = _out_projection(m, wo, xs, mod3, l, t, brow)

        h2, pe1, pthr, pe2 = _peer_select(xs, ln_ffn3, mod3, wqry, sk, l, t, brow)
        xs = _peer_experts(h2, u_tab, vt_tab, pe1, pthr, pe2, xs, mod3, l, t, brow)

    return xs[:n_lat].reshape(batch, seq, d)
```
